```python
import math
import jax
import jax.numpy as jnp
from jax import lax
import numpy as np

D_MODEL = 1024
BATCH = 8
SEQ = 4096
DEPTH = 2
DEC_BATCH = 128
DEC_SEQ = 4
PAST_LEN = 16384
PAGE_SIZE = 128

HEAD_DIM = 64
SWA_HEADS = 6
SWA_KV_HEADS = 2
SWA_GROUP = SWA_HEADS // SWA_KV_HEADS
SWA_Q = SWA_HEADS * HEAD_DIM
SWA_KV = SWA_KV_HEADS * HEAD_DIM
WINDOW = 128
ATTN_SCALE = HEAD_DIM ** -0.5
CONF_DIM = D_MODEL // 4
CONF_KERNEL = 31
GDN_HEADS = 4
GDN_DK = 64
GDN_DV = 64
GDN_DIM = GDN_HEADS * GDN_DK
GDN_CONV = 4
GDN_CHUNK = 64
POOL_WINDOWS = (2, 4, 8, 16)
POOL_GROUPS = len(POOL_WINDOWS)
POOL_DIM = D_MODEL // 4
POOL_GROUP_DIM = POOL_DIM // POOL_GROUPS
POOL_MAX = max(POOL_WINDOWS)
N_BRANCHES = 4
N_GROUPS = 4
EXPERTS_PER_GROUP = 8
N_EXPERTS = N_GROUPS * EXPERTS_PER_GROUP
TOP_K = 2
D_EXPERT = D_MODEL // 2
MOE_BLOCK = 128
EPS = 1e-6
IN_SPLITS = (SWA_Q, SWA_KV, SWA_KV, 2 * CONF_DIM, 3 * GDN_DIM, GDN_HEADS * GDN_DV, GDN_HEADS, GDN_HEADS, POOL_DIM, N_BRANCHES * D_MODEL)
IN_DIM = sum(IN_SPLITS)

kernel_name = 'hybrid_gated_parallel_mixers_hmoe_step'


def rms_norm(x, g):
    xf = x.astype(jnp.float32)
    y = xf * lax.rsqrt(jnp.mean(xf * xf, axis=-1, keepdims=True) + EPS)
    return (y * g.astype(jnp.float32)).astype(x.dtype)


def layer_norm(x, g, b):
    xf = x.astype(jnp.float32)
    xc = xf - jnp.mean(xf, axis=-1, keepdims=True)
    y = xc * lax.rsqrt(jnp.mean(xc * xc, axis=-1, keepdims=True) + EPS)
    return (y * g.astype(jnp.float32) + b.astype(jnp.float32)).astype(x.dtype)


def l2_normalize(x):
    return x * lax.rsqrt(jnp.sum(x * x, axis=-1, keepdims=True) + EPS)


def causal_depthwise_conv(x, buf, w):
    K, C = w.shape
    xp = jnp.concatenate([buf.astype(x.dtype), x], axis=1)
    y = lax.conv_general_dilated(xp, w.astype(x.dtype)[:, None, :], (1,), 'VALID',
                                 dimension_numbers=('NWC', 'WIO', 'NWC'), feature_group_count=C)
    return y, xp[:, -(K - 1):]


def sink_softmax(s, valid, sink):
    s = jnp.where(valid, s, -jnp.inf)
    m = jnp.maximum(jnp.max(s, axis=-1, keepdims=True), sink)
    p = jnp.exp(s - m)
    return p / (jnp.sum(p, axis=-1, keepdims=True) + jnp.exp(sink - m))


def swa_banded(q, k, v, sink):
    B, L = q.shape[0], q.shape[1]
    nb = L // WINDOW
    qb = q.reshape(B, nb, WINDOW, SWA_KV_HEADS, SWA_GROUP, HEAD_DIM)

    def with_prev_block(t):
        tb = t.reshape(B, nb, WINDOW, SWA_KV_HEADS, HEAD_DIM)
        prev = jnp.pad(tb, ((0, 0), (1, 0), (0, 0), (0, 0), (0, 0)))[:, :nb]
        return jnp.concatenate([prev, tb], axis=2)

    kk, vv = with_prev_block(k), with_prev_block(v)
    s = jnp.einsum('bnqhgd,bnkhd->bnhgqk', qb, kk, preferred_element_type=jnp.float32) * ATTN_SCALE
    qpos = WINDOW + jnp.arange(WINDOW)[:, None]
    kpos = jnp.arange(2 * WINDOW)[None, :]
    band = (kpos <= qpos) & (qpos - kpos < WINDOW)
    exists = (jnp.arange(nb)[:, None, None] > 0) | (kpos >= WINDOW)[None]
    valid = (band[None] & exists)[:, None, None]
    p = sink_softmax(s, valid, sink)
    o = jnp.einsum('bnhgqk,bnkhd->bnqhgd', p.astype(vv.dtype), vv)
    return o.reshape(B, L, SWA_Q), k[:, -WINDOW:], v[:, -WINDOW:]


def swa_with_buffer(q, k, v, past_k, past_v, sink):
    B, L = q.shape[0], q.shape[1]
    wbuf = past_k.shape[1]
    kk = jnp.concatenate([past_k.astype(k.dtype), k], axis=1)
    vv = jnp.concatenate([past_v.astype(v.dtype), v], axis=1)
    s = jnp.einsum('bqhgd,bkhd->bhgqk', q, kk, preferred_element_type=jnp.float32) * ATTN_SCALE
    qpos = wbuf + jnp.arange(L)[:, None]
    kpos = jnp.arange(wbuf + L)[None, :]
    valid = (kpos <= qpos) & (qpos - kpos < WINDOW)
    p = sink_softmax(s, valid, sink)
    o = jnp.einsum('bhgqk,bkhd->bqhgd', p.astype(vv.dtype), vv)
    return o.reshape(B, L, SWA_Q), kk[:, -wbuf:], vv[:, -wbuf:]


def gated_delta_rule(q, k, v, g, beta, s0):
    B, L, H, DK = q.shape
    DV = v.shape[-1]
    C = GDN_CHUNK if L % GDN_CHUNK == 0 else L
    n = L // C

    def blocks(t):
        t = t.reshape(B, n, C, H, *t.shape[3:])
        return jnp.moveaxis(t, (1, 2), (0, 3))

    qc, kc, vc = blocks(q), blocks(k), blocks(v)
    gc = jnp.cumsum(blocks(g), axis=-1)
    bc = blocks(beta)
    causal = jnp.tril(jnp.ones((C, C), bool))
    strict = jnp.tril(jnp.ones((C, C), bool), -1)
    diff = gc[..., :, None] - gc[..., None, :]
    decay = jnp.where(causal, jnp.exp(jnp.where(causal, diff, 0.0)), 0.0)
    kb = kc * bc[..., None]
    a = jnp.where(strict, jnp.einsum('nbhid,nbhjd->nbhij', kb, kc) * decay, 0.0)
    eye = jnp.eye(C, dtype=jnp.float32)
    t_inv = lax.linalg.triangular_solve(eye + a, jnp.broadcast_to(eye, a.shape), left_side=True,
                                        lower=True, unit_diagonal=True)
    u = t_inv @ (vc * bc[..., None])
    w = t_inv @ (kb * jnp.exp(gc)[..., None])
    qk = jnp.einsum('nbhid,nbhjd->nbhij', qc, kc) * decay
    qg = qc * jnp.exp(gc)[..., None]
    kd = kc * jnp.exp(gc[..., -1:] - gc)[..., None]
    g_tot = jnp.exp(gc[..., -1])

    def step(s, xs):
        qg_n, kd_n, u_n, w_n, qk_n, gt_n = xs
        v_new = u_n - jnp.einsum('bhcd,bhde->bhce', w_n, s)
        o = jnp.einsum('bhcd,bhde->bhce', qg_n, s) + jnp.einsum('bhij,bhje->bhie', qk_n, v_new)
        s = s * gt_n[..., None, None] + jnp.einsum('bhcd,bhce->bhde', kd_n, v_new)
        return s, o

    s_fin, o = lax.scan(step, s0, (qg, kd, u, w, qk, g_tot))
    o = jnp.moveaxis(o, (0, 3), (1, 2)).reshape(B, L, H, DV)
    return o, s_fin


def multiscale_pool(u, buf, start_pos, w_pool, scale):
    B, L, _ = u.shape
    P = POOL_MAX - 1
    up = jnp.concatenate([buf.astype(u.dtype), u], axis=1)
    cs = jnp.cumsum(jnp.pad(up.astype(jnp.float32), ((0, 0), (1, 0), (0, 0))), axis=1)
    end = cs[:, P + 1:]
    n_avail = (start_pos + jnp.arange(L) + 1)[None, :, None]
    means = []
    for gi, wlen in enumerate(POOL_WINDOWS):
        sl = slice(gi * POOL_GROUP_DIM, (gi + 1) * POOL_GROUP_DIM)
        win_sum = end[..., sl] - cs[:, P + 1 - wlen:P + 1 - wlen + L, sl]
        means.append(win_sum / jnp.minimum(n_avail, wlen).astype(jnp.float32))
    mixed = jnp.concatenate(means, axis=-1) - u.astype(jnp.float32)
    mixed = jnp.einsum('bsgc,gcd->bsgd', mixed.reshape(B, L, POOL_GROUPS, POOL_GROUP_DIM),
                       w_pool.astype(jnp.float32)).reshape(B, L, POOL_DIM)
    return (mixed * scale.astype(jnp.float32)).astype(u.dtype), up[:, -P:]


def token_mixer(xn, lp, state, start_pos):
    B, L, _ = xn.shape
    dt = xn.dtype
    offs = np.cumsum(IN_SPLITS)[:-1].tolist()
    (q, k, v, conf_in, gdn_qkv, gdn_z, gdn_b, gdn_a, pool_in, gate_in) = jnp.split(xn @ lp['w_in'], offs, axis=-1)
    past_k, past_v, conf_buf, gdn_buf, gdn_s, pool_buf = state

    q = q.reshape(B, L, SWA_KV_HEADS, SWA_GROUP, HEAD_DIM)
    k = k.reshape(B, L, SWA_KV_HEADS, HEAD_DIM)
    v = v.reshape(B, L, SWA_KV_HEADS, HEAD_DIM)
    sink = lp['swa_sinks'].astype(jnp.float32).reshape(SWA_KV_HEADS, SWA_GROUP)[:, :, None, None]
    if past_k is None:
        attn, new_k, new_v = swa_banded(q, k, v, sink)
    else:
        attn, new_k, new_v = swa_with_buffer(q, k, v, past_k, past_v, sink)

    glu_a, glu_b = jnp.split(conf_in, 2, axis=-1)
    cu = glu_a * jax.nn.sigmoid(glu_b)
    c, new_conf_buf = causal_depthwise_conv(cu, conf_buf, lp['conf_dw_w'])
    c = jax.nn.silu(layer_norm(c + lp['conf_dw_b'], lp['conf_ln_g'], lp['conf_ln_b']))

    qkv, new_gdn_buf = causal_depthwise_conv(gdn_qkv, gdn_buf, lp['gdn_conv_w'])
    qkv = jax.nn.silu(qkv.astype(jnp.float32))
    gq, gk, gv = jnp.split(qkv, 3, axis=-1)
    gq = l2_normalize(gq.reshape(B, L, GDN_HEADS, GDN_DK)) * (GDN_DK ** -0.5)
    gk = l2_normalize(gk.reshape(B, L, GDN_HEADS, GDN_DK))
    gv = gv.reshape(B, L, GDN_HEADS, GDN_DV)
    beta = jax.nn.sigmoid(gdn_b.astype(jnp.float32))
    log_decay = -jnp.exp(lp['gdn_a_log'].astype(jnp.float32)) * jax.nn.softplus(
        gdn_a.astype(jnp.float32) + lp['gdn_dt_bias'].astype(jnp.float32))
    o, new_s = gated_delta_rule(gq, gk, gv, log_decay, beta, gdn_s.astype(jnp.float32))
    o = rms_norm(o, lp['gdn_norm_g']) * jax.nn.silu(gdn_z.astype(jnp.float32).reshape(B, L, GDN_HEADS, GDN_DV))
    o = o.reshape(B, L, GDN_HEADS * GDN_DV).astype(dt)

    pooled, new_pool_buf = multiscale_pool(pool_in, pool_buf, start_pos, lp['pool_w'], lp['pool_scale'])

    gates = jax.nn.sigmoid(gate_in.astype(jnp.float32)).reshape(B, L, N_BRANCHES, D_MODEL)
    h = (gates[:, :, 0] * (attn @ lp['w_branch_attn'])
         + gates[:, :, 1] * (c @ lp['w_branch_conf'])
         + gates[:, :, 2] * (o @ lp['w_branch_gdn'])
         + gates[:, :, 3] * (pooled @ lp['w_branch_pool']))
    y = h.astype(dt) @ lp['w_out']
    return y, (new_k, new_v, new_conf_buf, new_gdn_buf, new_s.astype(dt), new_pool_buf)


def hier_route(x2, w_group, b_group, w_router, b_router):
    T = x2.shape[0]
    pg = jax.nn.softmax(jnp.dot(x2, w_group, preferred_element_type=jnp.float32) + b_group.astype(jnp.float32), axis=-1)
    p_grp, g_sel = lax.top_k(pg, 1)
    le = (jnp.dot(x2, w_router, preferred_element_type=jnp.float32) + b_router.astype(jnp.float32)).reshape(
        T, N_GROUPS, EXPERTS_PER_GROUP)
    idx = jnp.broadcast_to(g_sel[:, :, None], (T, 1, EXPERTS_PER_GROUP))
    le = jnp.take_along_axis(le, idx, axis=1)[:, 0]
    p_exp, e_sel = lax.top_k(jax.nn.softmax(le, axis=-1), TOP_K)
    weights = p_grp * p_exp / jnp.sum(p_exp, axis=-1, keepdims=True)
    return g_sel * EXPERTS_PER_GROUP + e_sel, weights


def moe_ffn(xn, lp):
    B, L, D = xn.shape
    x2 = xn.reshape(-1, D)
    T = x2.shape[0]
    experts, weights = hier_route(x2, lp['w_group'], lp['b_group'], lp['w_router'], lp['b_router'])
    S = T * TOP_K
    flat_e = experts.reshape(-1)
    order = jnp.argsort(flat_e)
    sorted_e = flat_e[order]
    counts = jnp.bincount(flat_e, length=N_EXPERTS)
    padded = (counts + MOE_BLOCK - 1) // MOE_BLOCK * MOE_BLOCK
    seg_start = jnp.cumsum(counts) - counts
    pad_end = jnp.cumsum(padded)
    pad_start = pad_end - padded
    dest = pad_start[sorted_e] + jnp.arange(S) - seg_start[sorted_e]
    n_blocks = (S + N_EXPERTS * (MOE_BLOCK - 1)) // MOE_BLOCK
    slot_token = order // TOP_K
    buf_token = jnp.zeros((n_blocks * MOE_BLOCK,), jnp.int32).at[dest].set(slot_token)
    block_expert = jnp.minimum(jnp.searchsorted(pad_end, jnp.arange(n_blocks) * MOE_BLOCK, side='right'),
                               N_EXPERTS - 1)
    xb = x2[buf_token].reshape(n_blocks, MOE_BLOCK, D)
    w_g, w_u, w_d = lp['w_exp_gate'], lp['w_exp_up'], lp['w_exp_down']

    def expert_block(args):
        xblk, e = args
        return (jax.nn.silu(xblk @ w_g[e]) * (xblk @ w_u[e])) @ w_d[e]

    yb = lax.map(expert_block, (xb, block_expert)).reshape(n_blocks * MOE_BLOCK, D)
    y_slots = yb[dest] * weights.reshape(-1)[order][:, None].astype(yb.dtype)
    out = jnp.zeros((T, D), yb.dtype).at[slot_token].add(y_slots)
    return out.reshape(B, L, D)


def run_trunk(x, past, start_pos, params):
    B = x.shape[0]
    collected = [[] for _ in range(6)]
    for l in range(DEPTH):
        lp = {name: arr[l] for name, arr in params.items()}
        if past is None:
            st = (None, None,
                  jnp.zeros((B, CONF_KERNEL - 1, CONF_DIM), x.dtype),
                  jnp.zeros((B, GDN_CONV - 1, 3 * GDN_DIM), x.dtype),
                  jnp.zeros((B, GDN_HEADS, GDN_DK, GDN_DV), jnp.float32),
                  jnp.zeros((B, POOL_MAX - 1, POOL_DIM), x.dtype))
        else:
            st = tuple(p[l] for p in past)
        y, new = token_mixer(rms_norm(x, lp['norm_mix_g']), lp, st, start_pos)
        x = x + y
        x = x + moe_ffn(rms_norm(x, lp['norm_ffn_g']), lp)
        for lst, arr in zip(collected, new):
            lst.append(arr)
    return x, [jnp.stack(lst, axis=0) for lst in collected]


def setup_inputs(seed: int = 0) -> dict:
    key = jax.random.key(seed)
    ks = iter(jax.random.split(key, 48))

    def nrm(shape, scale=1.0):
        return jax.random.normal(next(ks), shape, jnp.float32) * scale

    def gain(shape):
        return 1.0 + nrm(shape, 0.02)

    wbuf = min(WINDOW, PAST_LEN)
    dt_init = jnp.exp(jax.random.uniform(next(ks), (DEPTH, GDN_HEADS), jnp.float32,
                                         minval=math.log(1e-3), maxval=math.log(1e-1)))
    a_init = jax.random.uniform(next(ks), (DEPTH, GDN_HEADS), jnp.float32, minval=1.0, maxval=16.0)
    return {
        'x_prompt': nrm((BATCH, SEQ, D_MODEL)),
        'x_sample': nrm((DEC_BATCH, DEC_SEQ, D_MODEL)),
        'cache_swa_k': nrm((DEPTH, DEC_BATCH, wbuf, SWA_KV_HEADS, HEAD_DIM)),
        'cache_swa_v': nrm((DEPTH, DEC_BATCH, wbuf, SWA_KV_HEADS, HEAD_DIM)),
        'state_conf_conv': nrm((DEPTH, DEC_BATCH, CONF_KERNEL - 1, CONF_DIM), 0.5),
        'state_gdn_conv': nrm((DEPTH, DEC_BATCH, GDN_CONV - 1, 3 * GDN_DIM)),
        'state_gdn': nrm((DEPTH, DEC_BATCH, GDN_HEADS, GDN_DK, GDN_DV), 0.3),
        'state_pool': nrm((DEPTH, DEC_BATCH, POOL_MAX - 1, POOL_DIM)),
        'norm_mix_g': gain((DEPTH, D_MODEL)),
        'w_in': nrm((DEPTH, D_MODEL, IN_DIM), D_MODEL ** -0.5),
        'swa_sinks': nrm((DEPTH, SWA_HEADS), 0.5),
        'conf_dw_w': nrm((DEPTH, CONF_KERNEL, CONF_DIM), CONF_KERNEL ** -0.5),
        'conf_dw_b': nrm((DEPTH, CONF_DIM), 0.02),
        'conf_ln_g': gain((DEPTH, CONF_DIM)),
        'conf_ln_b': nrm((DEPTH, CONF_DIM), 0.02),
        'gdn_conv_w': nrm((DEPTH, GDN_CONV, 3 * GDN_DIM), GDN_CONV ** -0.5),
        'gdn_a_log': jnp.log(a_init),
        'gdn_dt_bias': dt_init + jnp.log(-jnp.expm1(-dt_init)),
        'gdn_norm_g': gain((DEPTH, GDN_DV)),
        'pool_w': nrm((DEPTH, POOL_GROUPS, POOL_GROUP_DIM, POOL_GROUP_DIM), POOL_GROUP_DIM ** -0.5),
        'pool_scale': 1.0 + nrm((DEPTH, POOL_DIM), 0.1),
        'w_branch_attn': nrm((DEPTH, SWA_Q, D_MODEL), SWA_Q ** -0.5),
        'w_branch_conf': nrm((DEPTH, CONF_DIM, D_MODEL), CONF_DIM ** -0.5),
        'w_branch_gdn': nrm((DEPTH, GDN_HEADS * GDN_DV, D_MODEL), (GDN_HEADS * GDN_DV) ** -0.5),
        'w_branch_pool': nrm((DEPTH, POOL_DIM, D_MODEL), POOL_DIM ** -0.5),
        'w_out': nrm((DEPTH, D_MODEL, D_MODEL), D_MODEL ** -0.5),
        'norm_ffn_g': gain((DEPTH, D_MODEL)),
        'w_group': nrm((DEPTH, D_MODEL, N_GROUPS), D_MODEL ** -0.5),
        'b_group': nrm((DEPTH, N_GROUPS), 0.01),
        'w_router': nrm((DEPTH, D_MODEL, N_EXPERTS), D_MODEL ** -0.5),
        'b_router': nrm((DEPTH, N_EXPERTS), 0.01),
        'w_exp_gate': nrm((DEPTH, N_EXPERTS, D_MODEL, D_EXPERT), D_MODEL ** -0.5),
        'w_exp_up': nrm((DEPTH, N_EXPERTS, D_MODEL, D_EXPERT), D_MODEL ** -0.5),
        'w_exp_down': nrm((DEPTH, N_EXPERTS, D_EXPERT, D_MODEL), D_EXPERT ** -0.5),
        'final_norm_g': gain((D_MODEL,)),
    }


def reference(x_prompt, x_sample, cache_swa_k, cache_swa_v, state_conf_conv, state_gdn_conv, state_gdn,
              state_pool, norm_mix_g, w_in, swa_sinks, conf_dw_w, conf_dw_b, conf_ln_g, conf_ln_b, gdn_conv_w,
              gdn_a_log, gdn_dt_bias, gdn_norm_g, pool_w, pool_scale, w_branch_attn, w_branch_conf, w_branch_gdn,
              w_branch_pool, w_out, norm_ffn_g, w_group, b_group, w_router, b_router, w_exp_gate, w_exp_up,
              w_exp_down, final_norm_g):
    params = dict(norm_mix_g=norm_mix_g, w_in=w_in, swa_sinks=swa_sinks, conf_dw_w=conf_dw_w,
                  conf_dw_b=conf_dw_b, conf_ln_g=conf_ln_g, conf_ln_b=conf_ln_b, gdn_conv_w=gdn_conv_w,
                  gdn_a_log=gdn_a_log, gdn_dt_bias=gdn_dt_bias, gdn_norm_g=gdn_norm_g, pool_w=pool_w,
                  pool_scale=pool_scale, w_branch_attn=w_branch_attn, w_branch_conf=w_branch_conf,
                  w_branch_gdn=w_branch_gdn, w_branch_pool=w_branch_pool, w_out=w_out, norm_ffn_g=norm_ffn_g,
                  w_group=w_group, b_group=b_group, w_router=w_router, b_router=b_router,
                  w_exp_gate=w_exp_gate, w_exp_up=w_exp_up, w_exp_down=w_exp_down)
    hp, (pk, pv, pc, pgc, pgs, pp) = run_trunk(x_prompt, None, 0, params)
    past = (cache_swa_k, cache_swa_v, state_conf_conv, state_gdn_conv, state_gdn, state_pool)
    hs, (sk, sv, sc, sgc, sgs, sp) = run_trunk(x_sample, past, PAST_LEN, params)
    y_prompt = rms_norm(hp, final_norm_g)
    y_sample = rms_norm(hs, final_norm_g)
    return (y_prompt, y_sample, pk, pv, pc, pgc, pgs, pp, sk, sv, sc, sgc, sgs, sp)
```

```python
import functools

import jax
import jax.numpy as jnp
from jax import lax
from jax.experimental import pallas as pl
from jax.experimental.pallas import tpu as pltpu

F32 = jnp.float32
BF16 = jnp.bfloat16
I32 = jnp.int32

D_MODEL = 1024
DEPTH = 2
PAST_LEN = 16384
HEAD_DIM = 64
SWA_HEADS = 6
SWA_KV_HEADS = 2
SWA_GROUP = SWA_HEADS // SWA_KV_HEADS
SWA_Q = SWA_HEADS * HEAD_DIM
SWA_KV = SWA_KV_HEADS * HEAD_DIM
WINDOW = 128
ATTN_SCALE = HEAD_DIM ** -0.5
CONF_DIM = D_MODEL // 4
CONF_KERNEL = 31
GDN_HEADS = 4
GDN_DK = 64
GDN_DV = 64
GDN_DIM = GDN_HEADS * GDN_DK
GDN_CONV = 4
POOL_WINDOWS = (2, 4, 8, 16)
POOL_DIM = D_MODEL // 4
POOL_GROUP_DIM = POOL_DIM // len(POOL_WINDOWS)
POOL_MAX = max(POOL_WINDOWS)
N_BRANCHES = 4
N_GROUPS = 4
EXPERTS_PER_GROUP = 8
N_EXPERTS = N_GROUPS * EXPERTS_PER_GROUP
D_EXPERT = D_MODEL // 2
MOE_BLOCK = 128
EPS = 1e-6

LANES = 128
CHUNK = 128
CONF_HIST = 32
POOL_HIST = 16
GDN_HIST = 8
VMEM_LIMIT = 56 * 1024 * 1024

PROJ_SPLITS = (SWA_Q, 2 * SWA_KV, 2 * CONF_DIM, 3 * GDN_DIM, GDN_HEADS * GDN_DV, POOL_DIM, LANES)
W_IN_OFFS = (0, SWA_Q, SWA_Q + SWA_KV, SWA_Q + 2 * SWA_KV)


def _params(*sem):
    return pltpu.CompilerParams(dimension_semantics=sem, vmem_limit_bytes=VMEM_LIMIT)


def _rms(x, g):
    return x * lax.rsqrt(jnp.mean(x * x, axis=-1, keepdims=True) + EPS) * g


def _sigmoid(x):
    return 1.0 / (1.0 + jnp.exp(-x))


def _dot(a, b):
    return jnp.dot(a, b, preferred_element_type=F32)


def _dot_nt(a, b):
    return lax.dot_general(a, b, (((1,), (1,)), ((), ())), preferred_element_type=F32)


def _split3(x):
    hi = x.astype(BF16)
    r = x - hi.astype(F32)
    mid = r.astype(BF16)
    lo = (r - mid.astype(F32)).astype(BF16)
    return hi, mid, lo


def _dot_exact_lhs(a_bf16, b):
    hi, mid, lo = _split3(b)
    return _dot(a_bf16, hi) + _dot(a_bf16, mid) + _dot(a_bf16, lo)


def _proj_body(x_ref, g_ref, w_ref, *out_refs):
    xn = _rms(x_ref[...], g_ref[...]).astype(BF16)
    off = 0
    for o_ref, n in zip(out_refs, PROJ_SPLITS):
        o_ref[...] = _dot(xn, w_ref[:, off:off + n])
        off += n


def _proj(x, g, w_small):
    T = x.shape[0]
    tm = min(512, T)
    n_small = sum(PROJ_SPLITS)
    return pl.pallas_call(
        _proj_body,
        grid=(T // tm,),
        in_specs=[pl.BlockSpec((tm, D_MODEL), lambda i: (i, 0)),
                  pl.BlockSpec((1, D_MODEL), lambda i: (0, 0)),
                  pl.BlockSpec((D_MODEL, n_small), lambda i: (0, 0))],
        out_specs=[pl.BlockSpec((tm, n), lambda i: (i, 0)) for n in PROJ_SPLITS],
        out_shape=[jax.ShapeDtypeStruct((T, n), F32) for n in PROJ_SPLITS],
        compiler_params=_params("parallel"),
        name="proj",
    )(x, g, w_small)


def _attn_body(has_past, sink_ref, q_ref, kv_ref, kvp_ref, kv0_ref, o_ref):
    i = pl.program_id(1)
    q = q_ref[...] * ATTN_SCALE
    kvc = kv_ref[...]
    kvp = jnp.where(i == 0, kv0_ref[...], kvp_ref[...])
    k2 = jnp.concatenate([kvp[:, :LANES], kvc[:, :LANES]], axis=0).astype(BF16)
    v2 = jnp.concatenate([kvp[:, LANES:], kvc[:, LANES:]], axis=0).astype(BF16)
    lane = lax.broadcasted_iota(I32, (WINDOW, LANES), 1)
    row = lax.broadcasted_iota(I32, (WINDOW, 2 * WINDOW), 0)
    col = lax.broadcasted_iota(I32, (WINDOW, 2 * WINDOW), 1)
    valid = (col > row) & (col <= row + WINDOW)
    if not has_past:
        valid = valid & ((col >= WINDOW) | (i > 0))
    for g in range(SWA_GROUP):
        qg = q[:, g * LANES:(g + 1) * LANES]
        og = jnp.zeros((WINDOW, LANES), F32)
        for j in range(SWA_KV_HEADS):
            half = (lane >= HEAD_DIM) if j == 1 else (lane < HEAD_DIM)
            qj = jnp.where(half, qg, 0.0).astype(BF16)
            s = jnp.where(valid, _dot_nt(qj, k2), -jnp.inf)
            sink = sink_ref[j * SWA_GROUP + g]
            m = jnp.maximum(jnp.max(s, axis=-1, keepdims=True), sink)
            p = jnp.exp(s - m)
            denom = jnp.sum(p, axis=-1, keepdims=True) + jnp.exp(sink - m)
            pv = _dot(p.astype(BF16), v2)
            og = jnp.where(half, pv / denom, og)
        o_ref[:, g * LANES:(g + 1) * LANES] = og.astype(BF16)


def _attn(q, kv, kv0, sinks, has_past):
    B, L, _ = q.shape
    nb = L // WINDOW
    return pl.pallas_call(
        functools.partial(_attn_body, has_past),
        grid=(B, nb),
        in_specs=[pl.BlockSpec(memory_space=pltpu.SMEM),
                  pl.BlockSpec((None, WINDOW, SWA_Q), lambda b, i: (b, i, 0)),
                  pl.BlockSpec((None, WINDOW, 2 * SWA_KV), lambda b, i: (b, i, 0)),
                  pl.BlockSpec((None, WINDOW, 2 * SWA_KV), lambda b, i: (b, jnp.maximum(i - 1, 0), 0)),
                  pl.BlockSpec((None, WINDOW, 2 * SWA_KV), lambda b, i: (b, 0, 0))],
        out_specs=pl.BlockSpec((None, WINDOW, SWA_Q), lambda b, i: (b, i, 0)),
        out_shape=jax.ShapeDtypeStruct((B, L, SWA_Q), BF16),
        compiler_params=_params("parallel", "arbitrary"),
        name="swa",
    )(sinks, q, kv, kv, kv0)


def _local_body(tl, start_pos, conf_ref, pool_ref, chist_ref, phist_ref, cw_ref, cb_ref, lng_ref, lnb_ref,
                pw_ref, ps_ref, c_ref, pooled_ref, culast_ref, cbuf, pbuf):
    l = pl.program_id(1)

    @pl.when(l == 0)
    def _():
        cbuf[0:CONF_HIST, :] = chist_ref[...]
        pbuf[0:POOL_HIST, :] = phist_ref[...]

    ci = conf_ref[...]
    cbuf[CONF_HIST:CONF_HIST + tl, :] = ci[:, :CONF_DIM] * _sigmoid(ci[:, CONF_DIM:])
    base = CONF_HIST - (CONF_KERNEL - 1)
    acc = cw_ref[0:1, :] * cbuf[base:base + tl, :]
    for j in range(1, CONF_KERNEL):
        acc = acc + cw_ref[j:j + 1, :] * cbuf[base + j:base + j + tl, :]
    y = acc + cb_ref[...]
    yc = y - jnp.mean(y, axis=-1, keepdims=True)
    yn = yc * lax.rsqrt(jnp.mean(yc * yc, axis=-1, keepdims=True) + EPS) * lng_ref[...] + lnb_ref[...]
    c_ref[...] = (yn * _sigmoid(yn)).astype(BF16)
    tail = cbuf[tl:tl + CONF_HIST, :]
    culast_ref[...] = tail
    cbuf[0:CONF_HIST, :] = tail

    u = pool_ref[...]
    pbuf[POOL_HIST:POOL_HIST + tl, :] = u
    lane = lax.broadcasted_iota(I32, (tl, POOL_DIM), 1)
    n_avail = (start_pos + 1 + l * tl + lax.broadcasted_iota(I32, (tl, POOL_DIM), 0)).astype(F32)
    wsum = u
    mean = None
    shift = 1
    for gi, wlen in enumerate(POOL_WINDOWS):
        while shift < wlen:
            wsum = wsum + pbuf[POOL_HIST - shift:POOL_HIST - shift + tl, :]
            shift += 1
        m = wsum / jnp.minimum(n_avail, float(wlen))
        mean = m if mean is None else jnp.where(lane >= gi * POOL_GROUP_DIM, m, mean)
    mixed = (mean - u).astype(BF16)
    pooled_ref[...] = (_dot(mixed, pw_ref[...]) * ps_ref[...]).astype(BF16)
    pbuf[0:POOL_HIST, :] = pbuf[tl:tl + POOL_HIST, :]


def _local_mixers(conf_in, pool_in, conf_hist, pool_hist, lw, start_pos):
    B, L, _ = conf_in.shape
    tl = min(512, L)
    row = lambda n: pl.BlockSpec((1, n), lambda b, l: (0, 0))
    return pl.pallas_call(
        functools.partial(_local_body, tl, start_pos),
        grid=(B, L // tl),
        in_specs=[pl.BlockSpec((None, tl, 2 * CONF_DIM), lambda b, l: (b, l, 0)),
                  pl.BlockSpec((None, tl, POOL_DIM), lambda b, l: (b, l, 0)),
                  pl.BlockSpec((None, CONF_HIST, CONF_DIM), lambda b, l: (b, 0, 0)),
                  pl.BlockSpec((None, POOL_HIST, POOL_DIM), lambda b, l: (b, 0, 0)),
                  pl.BlockSpec((CONF_HIST, CONF_DIM), lambda b, l: (0, 0)),
                  row(CONF_DIM), row(CONF_DIM), row(CONF_DIM),
                  pl.BlockSpec((POOL_DIM, POOL_DIM), lambda b, l: (0, 0)),
                  row(POOL_DIM)],
        out_specs=[pl.BlockSpec((None, tl, CONF_DIM), lambda b, l: (b, l, 0)),
                   pl.BlockSpec((None, tl, POOL_DIM), lambda b, l: (b, l, 0)),
                   pl.BlockSpec((None, CONF_HIST, CONF_DIM), lambda b, l: (b, 0, 0))],
        out_shape=[jax.ShapeDtypeStruct((B, L, CONF_DIM), BF16),
                   jax.ShapeDtypeStruct((B, L, POOL_DIM), BF16),
                   jax.ShapeDtypeStruct((B, CONF_HIST, CONF_DIM), F32)],
        scratch_shapes=[pltpu.VMEM((tl + CONF_HIST, CONF_DIM), F32),
                        pltpu.VMEM((tl + POOL_HIST, POOL_DIM), F32)],
        compiler_params=_params("parallel", "arbitrary"),
        name="conv_pool",
    )(conf_in, pool_in, conf_hist, pool_hist, lw["conf_w"], lw["conf_b"], lw["conf_ln_g"], lw["conf_ln_b"],
      lw["pool_w"], lw["pool_scale"])


def _unit_lower_inverse(a):
    row = lax.broadcasted_iota(I32, (CHUNK, CHUNK), 0)
    col = lax.broadcasted_iota(I32, (CHUNK, CHUNK), 1)
    eye = jnp.where(row == col, 1.0, 0.0)
    n = -a
    t = eye + n
    p = n
    for _ in range(CHUNK.bit_length() - 2):
        pb = p.astype(BF16)
        p = _dot(pb, pb)
        t = t + _dot(t.astype(BF16), p.astype(BF16))
    th, tm, _ = _split3(t)
    mh, mm, _ = _split3(eye + a)
    resid = eye - (_dot(mh, th) + (_dot(mh, tm) + _dot(mm, th)))
    return t + _dot(th, resid.astype(BF16))


def _gdn_body(n_valid, seq_len, gqkv_ref, gz_ref, gba_ref, hist_ref, s0_ref, cw_ref, alog_ref, dtb_ref, ng_ref,
              o_ref, sfin_ref, cbuf, s_scr):
    l = pl.program_id(1)

    @pl.when(l == 0)
    def _():
        cbuf[0:GDN_HIST, :] = hist_ref[...]
        s_scr[...] = s0_ref[...]

    cbuf[GDN_HIST:GDN_HIST + CHUNK, :] = gqkv_ref[...]
    base = GDN_HIST - (GDN_CONV - 1)
    acc = cw_ref[0:1, :] * cbuf[base:base + CHUNK, :]
    for j in range(1, GDN_CONV):
        acc = acc + cw_ref[j:j + 1, :] * cbuf[base + j:base + j + CHUNK, :]
    cbuf[0:GDN_HIST, :] = cbuf[CHUNK:CHUNK + GDN_HIST, :]
    qkv = acc * _sigmoid(acc)

    row = lax.broadcasted_iota(I32, (CHUNK, LANES), 0)
    col = lax.broadcasted_iota(I32, (CHUNK, LANES), 1)
    left = col < GDN_DK
    lower = row >= col
    strict = row > col
    blockdiag = (row < GDN_DK) == left

    def halfsum(x):
        sl = jnp.sum(jnp.where(left, x, 0.0), axis=-1, keepdims=True)
        sr = jnp.sum(jnp.where(left, 0.0, x), axis=-1, keepdims=True)
        return jnp.where(left, sl, sr)

    gba = gba_ref[...]
    beta_all = _sigmoid(gba)
    ga = gba + dtb_ref[...]
    softplus = jnp.maximum(ga, 0.0) + jnp.log(1.0 + jnp.exp(-jnp.abs(ga)))
    g_all = -jnp.exp(alog_ref[...]) * softplus
    if n_valid < seq_len:
        live = (l * CHUNK + row) < n_valid
        beta_all = jnp.where(live, beta_all, 0.0)
        g_all = jnp.where(live, g_all, 0.0)
    gc_all = _dot_exact_lhs(jnp.where(lower, 1.0, 0.0).astype(BF16), g_all)
    gc_t = gc_all.T
    g_last = gc_all[CHUNK - 1:CHUNK, :]
    e_gc = jnp.exp(gc_all)
    e_rest = jnp.exp(g_last - gc_all)
    e_tot = jnp.exp(g_last)

    def pair_cols(arr, lane0):
        return jnp.where(left, arr[:, lane0:lane0 + 1], arr[:, lane0 + 1:lane0 + 2])

    for p in range(GDN_HEADS // 2):
        h0 = 2 * p
        qp = qkv[:, p * LANES:(p + 1) * LANES]
        kp = qkv[:, GDN_DIM + p * LANES:GDN_DIM + (p + 1) * LANES]
        vp = qkv[:, 2 * GDN_DIM + p * LANES:2 * GDN_DIM + (p + 1) * LANES]
        qn = qp * lax.rsqrt(halfsum(qp * qp) + EPS) * (GDN_DK ** -0.5)
        kn = kp * lax.rsqrt(halfsum(kp * kp) + EPS)
        beta_p = pair_cols(beta_all, h0)
        egc_p = pair_cols(e_gc, GDN_HEADS + h0)
        erest_p = pair_cols(e_rest, GDN_HEADS + h0)
        kb = kn * beta_p
        rhs = jnp.concatenate([vp * beta_p, kb * egc_p], axis=1).astype(BF16)
        kn_b = kn.astype(BF16)
        res, qks = [], []
        for hh in range(2):
            h = h0 + hh
            hm = left if hh == 0 else jnp.logical_not(left)
            diff = gc_all[:, GDN_HEADS + h:GDN_HEADS + h + 1] - gc_t[GDN_HEADS + h:GDN_HEADS + h + 1, :]
            decay = jnp.where(lower, jnp.exp(jnp.where(lower, diff, 0.0)), 0.0)
            kk = _dot_nt(jnp.where(hm, kb, 0.0).astype(BF16), kn_b)
            qk = _dot_nt(jnp.where(hm, qn, 0.0).astype(BF16), kn_b) * decay
            t_inv = _unit_lower_inverse(jnp.where(strict, kk * decay, 0.0))
            res.append(_dot(t_inv.astype(BF16), rhs))
            qks.append(qk.astype(BF16))
        u = jnp.where(left, res[0][:, :LANES], res[1][:, :LANES])
        w = jnp.where(left, res[0][:, LANES:], res[1][:, LANES:])

        s = s_scr[p]
        sb = s.astype(BF16)
        v_new = u - _dot(w.astype(BF16), sb)
        vnb = v_new.astype(BF16)
        o = _dot((qn * egc_p).astype(BF16), sb) + jnp.where(left, _dot(qks[0], vnb), _dot(qks[1], vnb))
        kd_t = (kn * erest_p).T
        upd = _dot(kd_t.astype(BF16), vnb)
        gt = jnp.where(row < GDN_DK, e_tot[:, GDN_HEADS + h0:GDN_HEADS + h0 + 1],
                       e_tot[:, GDN_HEADS + h0 + 1:GDN_HEADS + h0 + 2])
        s_scr[p] = s * gt + jnp.where(blockdiag, upd, 0.0)

        on = o * lax.rsqrt(halfsum(o * o) * (1.0 / GDN_DV) + EPS) * ng_ref[...]
        z = gz_ref[:, p * LANES:(p + 1) * LANES]
        o_ref[:, p * LANES:(p + 1) * LANES] = (on * (z * _sigmoid(z))).astype(BF16)

    sfin_ref[...] = s_scr[...]


def _gdn(gqkv, gz, gba, hist, s0, lw, n_valid):
    B, L, _ = gqkv.shape
    row = lambda n: pl.BlockSpec((1, n), lambda b, l: (0, 0))
    return pl.pallas_call(
        functools.partial(_gdn_body, n_valid, L),
        grid=(B, L // CHUNK),
        in_specs=[pl.BlockSpec((None, CHUNK, 3 * GDN_DIM), lambda b, l: (b, l, 0)),
                  pl.BlockSpec((None, CHUNK, GDN_HEADS * GDN_DV), lambda b, l: (b, l, 0)),
                  pl.BlockSpec((None, CHUNK, LANES), lambda b, l: (b, l, 0)),
                  pl.BlockSpec((None, GDN_HIST, 3 * GDN_DIM), lambda b, l: (b, 0, 0)),
                  pl.BlockSpec((None, 2, LANES, LANES), lambda b, l: (b, 0, 0, 0)),
                  pl.BlockSpec((GDN_HIST, 3 * GDN_DIM), lambda b, l: (0, 0)),
                  row(LANES), row(LANES), row(LANES)],
        out_specs=[pl.BlockSpec((None, CHUNK, GDN_HEADS * GDN_DV), lambda b, l: (b, l, 0)),
                   pl.BlockSpec((None, 2, LANES, LANES), lambda b, l: (b, 0, 0, 0))],
        out_shape=[jax.ShapeDtypeStruct((B, L, GDN_HEADS * GDN_DV), BF16),
                   jax.ShapeDtypeStruct((B, 2, LANES, LANES), F32)],
        scratch_shapes=[pltpu.VMEM((CHUNK + GDN_HIST, 3 * GDN_DIM), F32),
                        pltpu.VMEM((2, LANES, LANES), F32)],
        compiler_params=_params("parallel", "arbitrary"),
        name="gdn",
    )(gqkv, gz, gba, hist, s0, lw["gdn_conv_w"], lw["gdn_a_log"], lw["gdn_dt_bias"], lw["gdn_norm_g"])


def _merge_body(tm, x_ref, a_ref, c_ref, o_ref, p_ref, gmix_ref, wg_ref, wa_ref, wc_ref, wo_ref, wp_ref,
                wout_ref, gffn_ref, wrh_ref, wrm_ref, wrl_ref, br_ref,
                xnew_ref, xn2_ref, ri_ref, rw_ref, cnt_ref, carry):
    i = pl.program_id(0)

    @pl.when(i == 0)
    def _():
        carry[...] = jnp.zeros_like(carry)

    x = x_ref[...]
    xn = _rms(x, gmix_ref[...]).astype(BF16)
    h = None
    for bi, (b_ref, w_ref) in enumerate(((a_ref, wa_ref), (c_ref, wc_ref), (o_ref, wo_ref), (p_ref, wp_ref))):
        gate = _sigmoid(_dot(xn, wg_ref[:, bi * D_MODEL:(bi + 1) * D_MODEL]))
        term = gate * _dot(b_ref[...], w_ref[...])
        h = term if h is None else h + term
    x_new = x + _dot(h.astype(BF16), wout_ref[...])
    xnew_ref[...] = x_new
    xn2 = _rms(x_new, gffn_ref[...])
    xn2_ref[...] = xn2

    xh, xm, xl = _split3(xn2)
    logits = (_dot(xh, wrh_ref[...]) + (_dot(xh, wrm_ref[...]) + _dot(xm, wrh_ref[...]))
              + (_dot(xh, wrl_ref[...]) + _dot(xl, wrh_ref[...]) + _dot(xm, wrm_ref[...]))) + br_ref[...]
    lane = lax.broadcasted_iota(I32, (tm, LANES), 1).astype(F32)
    neg = -jnp.inf
    lg = jnp.where(lane < N_GROUPS, logits, neg)
    mg = jnp.max(lg, axis=-1, keepdims=True)
    p_grp = 1.0 / jnp.sum(jnp.exp(lg - mg), axis=-1, keepdims=True)
    g_sel = jnp.min(jnp.where(lg == mg, lane, float(LANES)), axis=-1, keepdims=True)
    lo = N_GROUPS + EXPERTS_PER_GROUP * g_sel
    le = jnp.where((lane >= lo) & (lane < lo + EXPERTS_PER_GROUP), logits, neg)
    m1 = jnp.max(le, axis=-1, keepdims=True)
    i1 = jnp.min(jnp.where(le == m1, lane, float(LANES)), axis=-1, keepdims=True)
    le2 = jnp.where(lane == i1, neg, le)
    m2 = jnp.max(le2, axis=-1, keepdims=True)
    i2 = jnp.min(jnp.where(le2 == m2, lane, float(LANES)), axis=-1, keepdims=True)
    e2 = jnp.exp(m2 - m1)
    w1 = p_grp / (1.0 + e2)
    w2 = p_grp * e2 / (1.0 + e2)
    e1 = i1 - N_GROUPS
    e2i = i2 - N_GROUPS

    oh1 = lane == e1
    oh2 = lane == e2i
    oh = jnp.where(oh1, 1.0, jnp.where(oh2, 1.0, 0.0))
    r = lax.broadcasted_iota(I32, (tm, tm), 0)
    c = lax.broadcasted_iota(I32, (tm, tm), 1)
    before = _dot(jnp.where(r > c, 1.0, 0.0).astype(BF16), oh.astype(BF16)) + carry[...]
    rank1 = jnp.sum(jnp.where(oh1, before, 0.0), axis=-1, keepdims=True)
    rank2 = jnp.sum(jnp.where(oh2, before, 0.0), axis=-1, keepdims=True)
    carry[...] = carry[...] + jnp.sum(oh, axis=0, keepdims=True)
    cnt_ref[...] = jnp.broadcast_to(carry[...], cnt_ref.shape)
    ri = jnp.where(lane == 0, e1, jnp.where(lane == 1, e2i, jnp.where(lane == 2, rank1,
                                                                      jnp.where(lane == 3, rank2, 0.0))))
    ri_ref[...] = ri.astype(I32)
    rw_ref[...] = jnp.where(lane == 0, w1, jnp.where(lane == 1, w2, 0.0))


def _merge(x, attn, c, o, pooled, lw):
    T = x.shape[0]
    tm = min(512, T)
    tile = lambda n: pl.BlockSpec((tm, n), lambda i: (i, 0))
    full = lambda a: pl.BlockSpec(a.shape, lambda i: (0,) * a.ndim)
    weights = [lw["norm_mix_g"], lw["w_gate"], lw["w_branch_attn"], lw["w_branch_conf"], lw["w_branch_gdn"],
               lw["w_branch_pool"], lw["w_out"], lw["norm_ffn_g"], lw["w_route_hi"], lw["w_route_mid"],
               lw["w_route_lo"], lw["b_route"]]
    return pl.pallas_call(
        functools.partial(_merge_body, tm),
        grid=(T // tm,),
        in_specs=[tile(D_MODEL), tile(SWA_Q), tile(CONF_DIM), tile(GDN_HEADS * GDN_DV), tile(POOL_DIM)]
                 + [full(w) for w in weights],
        out_specs=[tile(D_MODEL), tile(D_MODEL), tile(LANES), tile(LANES),
                   pl.BlockSpec((8, LANES), lambda i: (0, 0))],
        out_shape=[jax.ShapeDtypeStruct((T, D_MODEL), F32), jax.ShapeDtypeStruct((T, D_MODEL), F32),
                   jax.ShapeDtypeStruct((T, LANES), I32), jax.ShapeDtypeStruct((T, LANES), F32),
                   jax.ShapeDtypeStruct((8, LANES), F32)],
        scratch_shapes=[pltpu.VMEM((1, LANES), F32)],
        compiler_params=_params("arbitrary"),
        name="merge_route",
    )(x, attn, c, o, pooled, *weights)


def _row_copy(src, s_row, dst, d_row, sem):
    return pltpu.make_async_copy(src.at[pl.ds(s_row, 1), :], dst.at[pl.ds(d_row, 1), :], sem)


def _dispatch_body(tc, n_blocks, dest_ref, pend_ref, x_ref, xb_ref, zbuf, sem_z, sem_s):
    i = pl.program_id(0)

    def zero_block(start):
        return pltpu.make_async_copy(zbuf, xb_ref.at[pl.ds(pl.multiple_of(start, MOE_BLOCK), MOE_BLOCK), :], sem_z)

    def zero_copy(e):
        return zero_block(jnp.maximum(pend_ref[e] - MOE_BLOCK, 0))

    @pl.when(i == 0)
    def _():
        zbuf[...] = jnp.zeros_like(zbuf)
        n_used = pend_ref[N_EXPERTS - 1] // MOE_BLOCK

        @pl.loop(0, N_EXPERTS)
        def _(e):
            zero_copy(e).start()

        @pl.loop(n_used, n_blocks)
        def _(b):
            zero_block(b * MOE_BLOCK).start()

        @pl.loop(0, N_EXPERTS)
        def _(e):
            zero_copy(e).wait()

        @pl.loop(n_used, n_blocks)
        def _(b):
            zero_block(b * MOE_BLOCK).wait()

    def copies(r):
        slot = (i * tc + r) * 2
        return (_row_copy(x_ref, r, xb_ref, dest_ref[slot], sem_s),
                _row_copy(x_ref, r, xb_ref, dest_ref[slot + 1], sem_s))

    @pl.loop(0, tc)
    def _(r):
        for cp in copies(r):
            cp.start()

    @pl.loop(0, tc)
    def _(r):
        for cp in copies(r):
            cp.wait()


def _dispatch(xn2, dest_flat, pad_end, n_rows):
    T = xn2.shape[0]
    tc = min(512, T)
    return pl.pallas_call(
        functools.partial(_dispatch_body, tc, n_rows // MOE_BLOCK),
        grid_spec=pltpu.PrefetchScalarGridSpec(
            num_scalar_prefetch=2,
            grid=(T // tc,),
            in_specs=[pl.BlockSpec((tc, D_MODEL), lambda i, d, p: (i, 0))],
            out_specs=pl.BlockSpec(memory_space=pl.ANY),
            scratch_shapes=[pltpu.VMEM((MOE_BLOCK, D_MODEL), F32), pltpu.SemaphoreType.DMA(()),
                            pltpu.SemaphoreType.DMA(())]),
        out_shape=jax.ShapeDtypeStruct((n_rows, D_MODEL), F32),
        compiler_params=_params("arbitrary"),
        name="moe_dispatch",
    )(dest_flat, pad_end, xn2)


def _expert_body(be_ref, nu_ref, x_ref, wg_ref, wu_ref, wd_ref, y_ref):
    i = pl.program_id(0)

    @pl.when(i < nu_ref[0])
    def _():
        x = x_ref[...].astype(BF16)
        a = _dot(x, wg_ref[...])
        h = (a * _sigmoid(a)) * _dot(x, wu_ref[...])
        y_ref[...] = _dot(h.astype(BF16), wd_ref[...])

    @pl.when(i >= nu_ref[0])
    def _():
        y_ref[...] = jnp.zeros_like(y_ref)


def _experts(xb, block_expert, n_used, lw):
    n_rows = xb.shape[0]
    n_blocks = n_rows // MOE_BLOCK
    return pl.pallas_call(
        _expert_body,
        grid_spec=pltpu.PrefetchScalarGridSpec(
            num_scalar_prefetch=2,
            grid=(n_blocks,),
            in_specs=[pl.BlockSpec((MOE_BLOCK, D_MODEL), lambda i, be, nu: (jnp.minimum(i, nu[0] - 1), 0)),
                      pl.BlockSpec((None, D_MODEL, D_EXPERT), lambda i, be, nu: (be[i], 0, 0)),
                      pl.BlockSpec((None, D_MODEL, D_EXPERT), lambda i, be, nu: (be[i], 0, 0)),
                      pl.BlockSpec((None, D_EXPERT, D_MODEL), lambda i, be, nu: (be[i], 0, 0))],
            out_specs=pl.BlockSpec((MOE_BLOCK, D_MODEL), lambda i, be, nu: (i, 0))),
        out_shape=jax.ShapeDtypeStruct((n_rows, D_MODEL), F32),
        compiler_params=_params("arbitrary"),
        name="moe_experts",
    )(block_expert, n_used, xb, lw["w_exp_gate"], lw["w_exp_up"], lw["w_exp_down"])


def _combine_body(tc, final, dest_ref, x_ref, w_ref, g_ref, yb_ref, out_ref, ybuf, sem):
    i = pl.program_id(0)

    def copies(r):
        slot = (i * tc + r) * 2
        return (_row_copy(yb_ref, dest_ref[slot], ybuf, r, sem),
                _row_copy(yb_ref, dest_ref[slot + 1], ybuf, tc + r, sem))

    @pl.loop(0, tc)
    def _(r):
        for cp in copies(r):
            cp.start()

    @pl.loop(0, tc)
    def _(r):
        for cp in copies(r):
            cp.wait()

    w = w_ref[...]
    out = x_ref[...] + (ybuf[0:tc, :] * w[:, 0:1] + ybuf[tc:2 * tc, :] * w[:, 1:2])
    if final:
        out = _rms(out, g_ref[...])
    out_ref[...] = out


def _combine(x_new, route_w, yb, dest_flat, g_final, final):
    T = x_new.shape[0]
    tc = min(256, T)
    return pl.pallas_call(
        functools.partial(_combine_body, tc, final),
        grid_spec=pltpu.PrefetchScalarGridSpec(
            num_scalar_prefetch=1,
            grid=(T // tc,),
            in_specs=[pl.BlockSpec((tc, D_MODEL), lambda i, d: (i, 0)),
                      pl.BlockSpec((tc, LANES), lambda i, d: (i, 0)),
                      pl.BlockSpec((1, D_MODEL), lambda i, d: (0, 0)),
                      pl.BlockSpec(memory_space=pl.ANY)],
            out_specs=pl.BlockSpec((tc, D_MODEL), lambda i, d: (i, 0)),
            scratch_shapes=[pltpu.VMEM((2 * tc, D_MODEL), F32), pltpu.SemaphoreType.DMA(())]),
        out_shape=jax.ShapeDtypeStruct((T, D_MODEL), F32),
        compiler_params=_params("arbitrary"),
        name="moe_combine",
    )(dest_flat, x_new, route_w, g_final, yb)


def _moe(x_new, xn2, route_i, route_w, counts_f, lw, g_final, final):
    T = x_new.shape[0]
    counts = counts_f[0, :N_EXPERTS].astype(I32)
    padded = (counts + MOE_BLOCK - 1) // MOE_BLOCK * MOE_BLOCK
    pad_end = jnp.cumsum(padded)
    pad_start = pad_end - padded
    eid = route_i[:, 0:2]
    onehot = eid[:, :, None] == jnp.arange(N_EXPERTS, dtype=I32)[None, None, :]
    dest = route_i[:, 2:4] + jnp.sum(jnp.where(onehot, pad_start[None, None, :], 0), axis=-1)
    dest_flat = dest.reshape(-1).astype(I32)
    n_blocks = (2 * T + N_EXPERTS * (MOE_BLOCK - 1)) // MOE_BLOCK
    n_used = (pad_end[N_EXPERTS - 1:] // MOE_BLOCK).astype(I32)
    block_start = jnp.arange(n_blocks, dtype=I32) * MOE_BLOCK
    block_expert = jnp.minimum(jnp.sum(block_start[:, None] >= pad_end[None, :], axis=-1), N_EXPERTS - 1).astype(I32)
    xb = _dispatch(xn2, dest_flat, pad_end.astype(I32), n_blocks * MOE_BLOCK)
    yb = _experts(xb, block_expert, n_used, lw)
    return _combine(x_new, route_w, yb, dest_flat, g_final, final)


def _layer_weights(p, l):
    w_in = p["w_in"][l]
    q_w = w_in[:, :SWA_Q].reshape(D_MODEL, SWA_KV_HEADS, SWA_GROUP, HEAD_DIM).transpose(0, 2, 1, 3)
    off = SWA_Q + 2 * SWA_KV
    conf_w = w_in[:, off:off + 2 * CONF_DIM]
    off += 2 * CONF_DIM
    gqkv_w = w_in[:, off:off + 3 * GDN_DIM]
    off += 3 * GDN_DIM
    gz_w = w_in[:, off:off + GDN_HEADS * GDN_DV]
    off += GDN_HEADS * GDN_DV
    gba_w = jnp.pad(w_in[:, off:off + 2 * GDN_HEADS], ((0, 0), (0, LANES - 2 * GDN_HEADS)))
    off += 2 * GDN_HEADS
    pool_w_in = w_in[:, off:off + POOL_DIM]
    off += POOL_DIM
    w_small = jnp.concatenate([q_w.reshape(D_MODEL, SWA_Q), w_in[:, SWA_Q:SWA_Q + 2 * SWA_KV], conf_w, gqkv_w, gz_w,
                               pool_w_in, gba_w], axis=1).astype(BF16)
    pool_bd = jnp.zeros((POOL_DIM, POOL_DIM), F32)
    for gi in range(len(POOL_WINDOWS)):
        sl = slice(gi * POOL_GROUP_DIM, (gi + 1) * POOL_GROUP_DIM)
        pool_bd = pool_bd.at[sl, sl].set(p["pool_w"][l, gi])
    lane_pad = lambda v: jnp.pad(v, (GDN_HEADS, LANES - 2 * GDN_HEADS))[None, :]
    w_route = jnp.pad(jnp.concatenate([p["w_group"][l], p["w_router"][l]], axis=1),
                      ((0, 0), (0, LANES - N_GROUPS - N_EXPERTS)))
    r_hi = w_route.astype(BF16)
    r_mid = (w_route - r_hi.astype(F32)).astype(BF16)
    r_lo = (w_route - r_hi.astype(F32) - r_mid.astype(F32)).astype(BF16)
    return {
        "norm_mix_g": p["norm_mix_g"][l][None, :],
        "w_small": w_small,
        "w_gate": w_in[:, off:].astype(BF16),
        "swa_sinks": p["swa_sinks"][l],
        "conf_w": jnp.pad(p["conf_dw_w"][l], ((0, CONF_HIST - CONF_KERNEL), (0, 0))),
        "conf_b": p["conf_dw_b"][l][None, :],
        "conf_ln_g": p["conf_ln_g"][l][None, :],
        "conf_ln_b": p["conf_ln_b"][l][None, :],
        "gdn_conv_w": jnp.pad(p["gdn_conv_w"][l], ((0, GDN_HIST - GDN_CONV), (0, 0))),
        "gdn_a_log": lane_pad(p["gdn_a_log"][l]),
        "gdn_dt_bias": lane_pad(p["gdn_dt_bias"][l]),
        "gdn_norm_g": jnp.tile(p["gdn_norm_g"][l], 2)[None, :],
        "pool_w": pool_bd.astype(BF16),
        "pool_scale": p["pool_scale"][l][None, :],
        "w_branch_attn": p["w_branch_attn"][l].reshape(SWA_KV_HEADS, SWA_GROUP, HEAD_DIM, D_MODEL)
                         .transpose(1, 0, 2, 3).reshape(SWA_Q, D_MODEL).astype(BF16),
        "w_branch_conf": p["w_branch_conf"][l].astype(BF16),
        "w_branch_gdn": p["w_branch_gdn"][l].astype(BF16),
        "w_branch_pool": p["w_branch_pool"][l].astype(BF16),
        "w_out": p["w_out"][l].astype(BF16),
        "norm_ffn_g": p["norm_ffn_g"][l][None, :],
        "w_route_hi": r_hi, "w_route_mid": r_mid, "w_route_lo": r_lo,
        "b_route": jnp.pad(jnp.concatenate([p["b_group"][l], p["b_router"][l]]),
                           (0, LANES - N_GROUPS - N_EXPERTS))[None, :],
        "w_exp_gate": p["w_exp_gate"][l].astype(BF16),
        "w_exp_up": p["w_exp_up"][l].astype(BF16),
        "w_exp_down": p["w_exp_down"][l].astype(BF16),
    }


def _pad_rows(a, n, front=False):
    extra = n - a.shape[1]
    return jnp.pad(a, ((0, 0), (extra, 0) if front else (0, extra), (0, 0)))


def _pair_state(s):
    z = jnp.zeros_like(s[:, 0::2])
    top = jnp.concatenate([s[:, 0::2], z], axis=-1)
    bot = jnp.concatenate([z, s[:, 1::2]], axis=-1)
    return jnp.concatenate([top, bot], axis=-2)


def _unpair_state(sp):
    even = sp[:, :, :GDN_DK, :GDN_DV]
    odd = sp[:, :, GDN_DK:, GDN_DV:]
    return jnp.stack([even, odd], axis=2).reshape(sp.shape[0], GDN_HEADS, GDN_DK, GDN_DV)


def _token_mixer(x, lw, state, start_pos):
    B, L, _ = x.shape
    T = B * L
    q, kv, conf_in, gqkv, gz, pool_in, gba = _proj(x.reshape(T, D_MODEL), lw["norm_mix_g"], lw["w_small"])
    seq = lambda a: a.reshape(B, L, a.shape[-1])
    q, kv, conf_in, gqkv, gz, pool_in, gba = map(seq, (q, kv, conf_in, gqkv, gz, pool_in, gba))
    has_past = state is not None
    if has_past:
        past_k, past_v, conf_buf, gdn_buf, gdn_s, pool_buf = state
        kv0 = jnp.concatenate([past_k.reshape(B, WINDOW, SWA_KV), past_v.reshape(B, WINDOW, SWA_KV)], axis=-1)
        lq, ll, lg = WINDOW, 8, CHUNK
    else:
        conf_buf = jnp.zeros((B, CONF_KERNEL - 1, CONF_DIM), F32)
        gdn_buf = jnp.zeros((B, GDN_CONV - 1, 3 * GDN_DIM), F32)
        gdn_s = jnp.zeros((B, GDN_HEADS, GDN_DK, GDN_DV), F32)
        pool_buf = jnp.zeros((B, POOL_MAX - 1, POOL_DIM), F32)
        kv0 = jnp.zeros((B, WINDOW, 2 * SWA_KV), F32)
        lq = ll = lg = L
    attn = _attn(_pad_rows(q, lq), _pad_rows(kv, lq), kv0, lw["swa_sinks"], has_past)[:, :L]
    c, pooled, cu_last = _local_mixers(_pad_rows(conf_in, ll), _pad_rows(pool_in, ll),
                                       _pad_rows(conf_buf, CONF_HIST, front=True),
                                       _pad_rows(pool_buf, POOL_HIST, front=True), lw, start_pos)
    o, s_fin = _gdn(_pad_rows(gqkv, lg), _pad_rows(gz, lg), _pad_rows(gba, lg),
                    _pad_rows(gdn_buf, GDN_HIST, front=True), _pair_state(gdn_s), lw, L)
    if has_past:
        new_kv = jnp.concatenate([kv0[:, L:], kv], axis=1)
        n_hist = cu_last.shape[1]
        new_conf = jnp.concatenate([conf_buf[:, L:], cu_last[:, n_hist - ll:n_hist - ll + L]], axis=1)
        new_gdn_buf = jnp.concatenate([gdn_buf, gqkv], axis=1)[:, -(GDN_CONV - 1):]
        new_pool = jnp.concatenate([pool_buf[:, L:], pool_in], axis=1)
    else:
        new_kv = kv[:, -WINDOW:]
        new_conf = cu_last[:, -(CONF_KERNEL - 1):]
        new_gdn_buf = gqkv[:, -(GDN_CONV - 1):]
        new_pool = pool_in[:, -(POOL_MAX - 1):]
    new_k = new_kv[..., :SWA_KV].reshape(B, WINDOW, SWA_KV_HEADS, HEAD_DIM)
    new_v = new_kv[..., SWA_KV:].reshape(B, WINDOW, SWA_KV_HEADS, HEAD_DIM)
    flat = lambda a: a[:, :L].reshape(T, a.shape[-1])
    branches = (flat(attn), flat(c), flat(o), flat(pooled))
    return branches, (new_k, new_v, new_conf, new_gdn_buf, _unpair_state(s_fin), new_pool)


def _trunk(x, past, start_pos, layers, g_final):
    B, L, _ = x.shape
    collected = [[] for _ in range(6)]
    xf = x.reshape(B * L, D_MODEL)
    for l, lw in enumerate(layers):
        state = None if past is None else tuple(p[l] for p in past)
        branches, new = _token_mixer(xf.reshape(B, L, D_MODEL), lw, state, start_pos)
        x_new, xn2, route_i, route_w, counts = _merge(xf, *branches, lw)
        xf = _moe(x_new, xn2, route_i, route_w, counts, lw, g_final, l == len(layers) - 1)
        for lst, arr in zip(collected, new):
            lst.append(arr)
    return xf.reshape(B, L, D_MODEL), [jnp.stack(lst, axis=0) for lst in collected]


def kernel(x_prompt, x_sample, cache_swa_k, cache_swa_v, state_conf_conv, state_gdn_conv, state_gdn, state_pool, norm_mix_g, w_in, swa_sinks, conf_dw_w, conf_dw_b, conf_ln_g, conf_ln_b, gdn_conv_w, gdn_a_log, gdn_dt_bias, gdn_norm_g, pool_w, pool_scale, w_branch_attn, w_branch_conf, w_branch_gdn, w_branch_pool, w_out, norm_ffn_g, w_group, b_group, w_router, b_router, w_exp_gate, w_exp_up, w_exp_down, final_norm_g):
    params = dict(norm_mix_g=norm_mix_g, w_in=w_in, swa_sinks=swa_sinks, conf_dw_w=conf_dw_w,
                  conf_dw_b=conf_dw_b, conf_ln_g=conf_ln_g, conf_ln_b=conf_ln_b, gdn_conv_w=gdn_conv_w,
                  gdn_a_log=gdn_a_log, gdn_dt_bias=gdn_dt_bias, gdn_norm_g=gdn_norm_g, pool_w=pool_w,
                  pool_scale=pool_scale, w_branch_attn=w_branch_attn, w_branch_conf=w_branch_conf,
                  w_branch_gdn=w_branch_gdn, w_branch_pool=w_branch_pool, w_out=w_out, norm_ffn_g=norm_ffn_g,
                  w_group=w_group, b_group=b_group, w_router=w_router, b_router=b_router,
                  w_exp_gate=w_exp_gate, w_exp_up=w_exp_up, w_exp_down=w_exp_down)
    layers = [_layer_weights(params, l) for l in range(w_in.shape[0])]
    g_final = final_norm_g[None, :]
    yp, (pk, pv, pc, pgc, pgs, pp) = _trunk(x_prompt, None, 0, layers, g_final)
    past = (cache_swa_k, cache_swa_v, state_conf_conv, state_gdn_conv, state_gdn, state_pool)
    ys, (sk, sv, sc, sgc, sgs, sp) = _trunk(x_sample, past, PAST_LEN, layers, g_final)
    return (yp, ys, pk, pv, pc, pgc, pgs, pp, sk, sv, sc, sgc, sgs, sp)
```

```python
import functools

import jax
import jax.numpy as jnp
from jax import lax
from jax.experimental import pallas as pl
from jax.experimental.pallas import tpu as pltpu

F32 = jnp.float32
BF16 = jnp.bfloat16
I32 = jnp.int32

D_MODEL = 1024
DEPTH = 2
PAST_LEN = 16384
HEAD_DIM = 64
SWA_HEADS = 6
SWA_KV_HEADS = 2
SWA_GROUP = SWA_HEADS // SWA_KV_HEADS
SWA_Q = SWA_HEADS * HEAD_DIM
SWA_KV = SWA_KV_HEADS * HEAD_DIM
WINDOW = 128
ATTN_SCALE = HEAD_DIM ** -0.5
CONF_DIM = D_MODEL // 4
CONF_KERNEL = 31
GDN_HEADS = 4
GDN_DK = 64
GDN_DV = 64
GDN_DIM = GDN_HEADS * GDN_DK
GDN_CONV = 4
POOL_WINDOWS = (2, 4, 8, 16)
POOL_DIM = D_MODEL // 4
POOL_GROUP_DIM = POOL_DIM // len(POOL_WINDOWS)
POOL_MAX = max(POOL_WINDOWS)
N_BRANCHES = 4
N_GROUPS = 4
EXPERTS_PER_GROUP = 8
N_EXPERTS = N_GROUPS * EXPERTS_PER_GROUP
D_EXPERT = D_MODEL // 2
MOE_BLOCK = 256
EPS = 1e-6

LANES = 128
CHUNK = 128
CONF_HIST = 32
POOL_HIST = 16
GDN_HIST = 8
GDN_ROWS = 4
VMEM_LIMIT = 56 * 1024 * 1024

PROJ_SPLITS = (SWA_Q, 2 * SWA_KV, 2 * CONF_DIM, 3 * GDN_DIM, GDN_HEADS * GDN_DV, POOL_DIM, LANES)
W_IN_OFFS = (0, SWA_Q, SWA_Q + SWA_KV, SWA_Q + 2 * SWA_KV)


def _params(*sem):
    return pltpu.CompilerParams(dimension_semantics=sem, vmem_limit_bytes=VMEM_LIMIT)


def _rms(x, g):
    return x * lax.rsqrt(jnp.mean(x * x, axis=-1, keepdims=True) + EPS) * g


def _sigmoid(x):
    return 1.0 / (1.0 + jnp.exp(-x))


def _dot(a, b):
    return jnp.dot(a, b, preferred_element_type=F32)


def _dot_nt(a, b):
    return lax.dot_general(a, b, (((1,), (1,)), ((), ())), preferred_element_type=F32)


def _split3(x):
    hi = x.astype(BF16)
    r = x - hi.astype(F32)
    mid = r.astype(BF16)
    lo = (r - mid.astype(F32)).astype(BF16)
    return hi, mid, lo


def _dot_exact_lhs(a_bf16, b):
    hi, mid, lo = _split3(b)
    return _dot(a_bf16, hi) + _dot(a_bf16, mid) + _dot(a_bf16, lo)


def _proj_body(x_ref, g_ref, w_ref, *out_refs):
    xn = _rms(x_ref[...], g_ref[...]).astype(BF16)
    off = 0
    for o_ref, n in zip(out_refs, PROJ_SPLITS):
        o_ref[...] = _dot(xn, w_ref[:, off:off + n])
        off += n


def _proj(x, g, w_small):
    T = x.shape[0]
    tm = min(512, T)
    n_small = sum(PROJ_SPLITS)
    return pl.pallas_call(
        _proj_body,
        grid=(T // tm,),
        in_specs=[pl.BlockSpec((tm, D_MODEL), lambda i: (i, 0)),
                  pl.BlockSpec((1, D_MODEL), lambda i: (0, 0)),
                  pl.BlockSpec((D_MODEL, n_small), lambda i: (0, 0))],
        out_specs=[pl.BlockSpec((tm, n), lambda i: (i, 0)) for n in PROJ_SPLITS],
        out_shape=[jax.ShapeDtypeStruct((T, n), F32) for n in PROJ_SPLITS],
        compiler_params=_params("parallel"),
        name="proj",
    )(x, g, w_small)


def _attn_body(has_past, sink_ref, q_ref, kv_ref, kvp_ref, kv0_ref, o_ref):
    i = pl.program_id(1)
    q = q_ref[...] * ATTN_SCALE
    kvc = kv_ref[...]
    kvp = jnp.where(i == 0, kv0_ref[...], kvp_ref[...])
    k2 = jnp.concatenate([kvp[:, :LANES], kvc[:, :LANES]], axis=0).astype(BF16)
    v2 = jnp.concatenate([kvp[:, LANES:], kvc[:, LANES:]], axis=0).astype(BF16)
    lane = lax.broadcasted_iota(I32, (WINDOW, LANES), 1)
    row = lax.broadcasted_iota(I32, (WINDOW, 2 * WINDOW), 0)
    col = lax.broadcasted_iota(I32, (WINDOW, 2 * WINDOW), 1)
    valid = (col > row) & (col <= row + WINDOW)
    if not has_past:
        valid = valid & ((col >= WINDOW) | (i > 0))
    for g in range(SWA_GROUP):
        qg = q[:, g * LANES:(g + 1) * LANES]
        og = jnp.zeros((WINDOW, LANES), F32)
        for j in range(SWA_KV_HEADS):
            half = (lane >= HEAD_DIM) if j == 1 else (lane < HEAD_DIM)
            qj = jnp.where(half, qg, 0.0).astype(BF16)
            s = jnp.where(valid, _dot_nt(qj, k2), -jnp.inf)
            sink = sink_ref[j * SWA_GROUP + g]
            m = jnp.maximum(jnp.max(s, axis=-1, keepdims=True), sink)
            p = jnp.exp(s - m)
            denom = jnp.sum(p, axis=-1, keepdims=True) + jnp.exp(sink - m)
            pv = _dot(p.astype(BF16), v2)
            og = jnp.where(half, pv / denom, og)
        o_ref[:, g * LANES:(g + 1) * LANES] = og.astype(BF16)


def _attn(q, kv, kv0, sinks, has_past):
    B, L, _ = q.shape
    nb = L // WINDOW
    return pl.pallas_call(
        functools.partial(_attn_body, has_past),
        grid=(B, nb),
        in_specs=[pl.BlockSpec(memory_space=pltpu.SMEM),
                  pl.BlockSpec((None, WINDOW, SWA_Q), lambda b, i: (b, i, 0)),
                  pl.BlockSpec((None, WINDOW, 2 * SWA_KV), lambda b, i: (b, i, 0)),
                  pl.BlockSpec((None, WINDOW, 2 * SWA_KV), lambda b, i: (b, jnp.maximum(i - 1, 0), 0)),
                  pl.BlockSpec((None, WINDOW, 2 * SWA_KV), lambda b, i: (b, 0, 0))],
        out_specs=pl.BlockSpec((None, WINDOW, SWA_Q), lambda b, i: (b, i, 0)),
        out_shape=jax.ShapeDtypeStruct((B, L, SWA_Q), BF16),
        compiler_params=_params("parallel", "arbitrary"),
        name="swa",
    )(sinks, q, kv, kv, kv0)


def _local_body(tl, start_pos, conf_ref, pool_ref, chist_ref, phist_ref, cw_ref, cb_ref, lng_ref, lnb_ref,
                pw_ref, ps_ref, c_ref, pooled_ref, culast_ref, cbuf, pbuf):
    l = pl.program_id(1)

    @pl.when(l == 0)
    def _():
        cbuf[0:CONF_HIST, :] = chist_ref[...]
        pbuf[0:POOL_HIST, :] = phist_ref[...]

    ci = conf_ref[...]
    cbuf[CONF_HIST:CONF_HIST + tl, :] = ci[:, :CONF_DIM] * _sigmoid(ci[:, CONF_DIM:])
    base = CONF_HIST - (CONF_KERNEL - 1)
    acc = cw_ref[0:1, :] * cbuf[base:base + tl, :]
    for j in range(1, CONF_KERNEL):
        acc = acc + cw_ref[j:j + 1, :] * cbuf[base + j:base + j + tl, :]
    y = acc + cb_ref[...]
    yc = y - jnp.mean(y, axis=-1, keepdims=True)
    yn = yc * lax.rsqrt(jnp.mean(yc * yc, axis=-1, keepdims=True) + EPS) * lng_ref[...] + lnb_ref[...]
    c_ref[...] = (yn * _sigmoid(yn)).astype(BF16)
    tail = cbuf[tl:tl + CONF_HIST, :]
    culast_ref[...] = tail
    cbuf[0:CONF_HIST, :] = tail

    u = pool_ref[...]
    pbuf[POOL_HIST:POOL_HIST + tl, :] = u
    lane = lax.broadcasted_iota(I32, (tl, POOL_DIM), 1)
    n_avail = (start_pos + 1 + l * tl + lax.broadcasted_iota(I32, (tl, POOL_DIM), 0)).astype(F32)
    wsum = u
    mean = None
    shift = 1
    for gi, wlen in enumerate(POOL_WINDOWS):
        while shift < wlen:
            wsum = wsum + pbuf[POOL_HIST - shift:POOL_HIST - shift + tl, :]
            shift += 1
        m = wsum / jnp.minimum(n_avail, float(wlen))
        mean = m if mean is None else jnp.where(lane >= gi * POOL_GROUP_DIM, m, mean)
    mixed = (mean - u).astype(BF16)
    pooled_ref[...] = (_dot(mixed, pw_ref[...]) * ps_ref[...]).astype(BF16)
    pbuf[0:POOL_HIST, :] = pbuf[tl:tl + POOL_HIST, :]


def _local_mixers(conf_in, pool_in, conf_hist, pool_hist, lw, start_pos):
    B, L, _ = conf_in.shape
    tl = min(512, L)
    row = lambda n: pl.BlockSpec((1, n), lambda b, l: (0, 0))
    return pl.pallas_call(
        functools.partial(_local_body, tl, start_pos),
        grid=(B, L // tl),
        in_specs=[pl.BlockSpec((None, tl, 2 * CONF_DIM), lambda b, l: (b, l, 0)),
                  pl.BlockSpec((None, tl, POOL_DIM), lambda b, l: (b, l, 0)),
                  pl.BlockSpec((None, CONF_HIST, CONF_DIM), lambda b, l: (b, 0, 0)),
                  pl.BlockSpec((None, POOL_HIST, POOL_DIM), lambda b, l: (b, 0, 0)),
                  pl.BlockSpec((CONF_HIST, CONF_DIM), lambda b, l: (0, 0)),
                  row(CONF_DIM), row(CONF_DIM), row(CONF_DIM),
                  pl.BlockSpec((POOL_DIM, POOL_DIM), lambda b, l: (0, 0)),
                  row(POOL_DIM)],
        out_specs=[pl.BlockSpec((None, tl, CONF_DIM), lambda b, l: (b, l, 0)),
                   pl.BlockSpec((None, tl, POOL_DIM), lambda b, l: (b, l, 0)),
                   pl.BlockSpec((None, CONF_HIST, CONF_DIM), lambda b, l: (b, 0, 0))],
        out_shape=[jax.ShapeDtypeStruct((B, L, CONF_DIM), BF16),
                   jax.ShapeDtypeStruct((B, L, POOL_DIM), BF16),
                   jax.ShapeDtypeStruct((B, CONF_HIST, CONF_DIM), F32)],
        scratch_shapes=[pltpu.VMEM((tl + CONF_HIST, CONF_DIM), F32),
                        pltpu.VMEM((tl + POOL_HIST, POOL_DIM), F32)],
        compiler_params=_params("parallel", "arbitrary"),
        name="conv_pool",
    )(conf_in, pool_in, conf_hist, pool_hist, lw["conf_w"], lw["conf_b"], lw["conf_ln_g"], lw["conf_ln_b"],
      lw["pool_w"], lw["pool_scale"])


def _unit_lower_inverses(mats):
    row = lax.broadcasted_iota(I32, (CHUNK, CHUNK), 0)
    col = lax.broadcasted_iota(I32, (CHUNK, CHUNK), 1)
    eye = jnp.where(row == col, 1.0, 0.0)
    ps = [-a for a in mats]
    ts = [eye + n for n in ps]
    for _ in range(CHUNK.bit_length() - 2):
        pbs = [p.astype(BF16) for p in ps]
        ps = [_dot(pb, pb) for pb in pbs]
        ts = [t + _dot(t.astype(BF16), p.astype(BF16)) for t, p in zip(ts, ps)]
    splits = [(_split3(t), _split3(eye + a)) for t, a in zip(ts, mats)]
    resids = [eye - (_dot(mh, th) + (_dot(mh, tm) + _dot(mm, th))) for (th, tm, _), (mh, mm, _) in splits]
    return [t + _dot(sp[0][0], r.astype(BF16)) for t, sp, r in zip(ts, splits, resids)]


def _gdn_body(n_valid, seq_len, gqkv_ref, gz_ref, gba_ref, hist_ref, s0_ref, cw_ref, alog_ref, dtb_ref, ng_ref,
              o_ref, sfin_ref, cbuf, s_scr):
    l = pl.program_id(1)
    rows = range(GDN_ROWS)
    pairs = [(g, p) for g in rows for p in range(GDN_HEADS // 2)]
    heads = [(g, p, hh) for (g, p) in pairs for hh in range(2)]

    @pl.when(l == 0)
    def _():
        cbuf[:, 0:GDN_HIST, :] = hist_ref[...]
        s_scr[...] = s0_ref[...]

    row = lax.broadcasted_iota(I32, (CHUNK, LANES), 0)
    col = lax.broadcasted_iota(I32, (CHUNK, LANES), 1)
    left = col < GDN_DK
    right = jnp.logical_not(left)
    lower = row >= col
    strict = row > col
    blockdiag = (row < GDN_DK) == left

    def halfsum(x):
        sl = jnp.sum(jnp.where(left, x, 0.0), axis=-1, keepdims=True)
        sr = jnp.sum(jnp.where(left, 0.0, x), axis=-1, keepdims=True)
        return jnp.where(left, sl, sr)

    def pair_cols(arr, lane0):
        return jnp.where(left, arr[:, lane0:lane0 + 1], arr[:, lane0 + 1:lane0 + 2])

    base = GDN_HIST - (GDN_CONV - 1)
    qkv = {}
    for g in rows:
        cbuf[g, GDN_HIST:GDN_HIST + CHUNK, :] = gqkv_ref[g]
        acc = cw_ref[0:1, :] * cbuf[g, base:base + CHUNK, :]
        for j in range(1, GDN_CONV):
            acc = acc + cw_ref[j:j + 1, :] * cbuf[g, base + j:base + j + CHUNK, :]
        cbuf[g, 0:GDN_HIST, :] = cbuf[g, CHUNK:CHUNK + GDN_HIST, :]
        qkv[g] = acc * _sigmoid(acc)

    ltri = jnp.where(lower, 1.0, 0.0).astype(BF16)
    beta_all, g_parts = {}, {}
    for g in rows:
        gba = gba_ref[g]
        beta = _sigmoid(gba)
        ga = gba + dtb_ref[...]
        softplus = jnp.maximum(ga, 0.0) + jnp.log(1.0 + jnp.exp(-jnp.abs(ga)))
        gl = -jnp.exp(alog_ref[...]) * softplus
        if n_valid < seq_len:
            live = (l * CHUNK + row) < n_valid
            beta = jnp.where(live, beta, 0.0)
            gl = jnp.where(live, gl, 0.0)
        beta_all[g] = beta
        g_parts[g] = _split3(gl)
    gc_all = {g: _dot(ltri, g_parts[g][0]) + _dot(ltri, g_parts[g][1]) + _dot(ltri, g_parts[g][2]) for g in rows}
    gc_t = {g: gc_all[g].T for g in rows}
    g_last = {g: gc_all[g][CHUNK - 1:CHUNK, :] for g in rows}
    e_gc = {g: jnp.exp(gc_all[g]) for g in rows}
    e_rest = {g: jnp.exp(g_last[g] - gc_all[g]) for g in rows}
    e_tot = {g: jnp.exp(g_last[g]) for g in rows}

    qn, kn, kn_b, kb, rhs, egc_p, erest_p = {}, {}, {}, {}, {}, {}, {}
    for (g, p) in pairs:
        h0 = 2 * p
        qp = qkv[g][:, p * LANES:(p + 1) * LANES]
        kp = qkv[g][:, GDN_DIM + p * LANES:GDN_DIM + (p + 1) * LANES]
        vp = qkv[g][:, 2 * GDN_DIM + p * LANES:2 * GDN_DIM + (p + 1) * LANES]
        qn[g, p] = qp * lax.rsqrt(halfsum(qp * qp) + EPS) * (GDN_DK ** -0.5)
        kn[g, p] = kp * lax.rsqrt(halfsum(kp * kp) + EPS)
        beta_p = pair_cols(beta_all[g], h0)
        egc_p[g, p] = pair_cols(e_gc[g], GDN_HEADS + h0)
        erest_p[g, p] = pair_cols(e_rest[g], GDN_HEADS + h0)
        kb[g, p] = kn[g, p] * beta_p
        rhs[g, p] = jnp.concatenate([vp * beta_p, kb[g, p] * egc_p[g, p]], axis=1).astype(BF16)
        kn_b[g, p] = kn[g, p].astype(BF16)

    decay, kk, qk = {}, {}, {}
    for (g, p, hh) in heads:
        lane = GDN_HEADS + 2 * p + hh
        hm = left if hh == 0 else right
        diff = gc_all[g][:, lane:lane + 1] - gc_t[g][lane:lane + 1, :]
        decay[g, p, hh] = jnp.where(lower, jnp.exp(jnp.where(lower, diff, 0.0)), 0.0)
        kk[g, p, hh] = _dot_nt(jnp.where(hm, kb[g, p], 0.0).astype(BF16), kn_b[g, p])
        qk[g, p, hh] = _dot_nt(jnp.where(hm, qn[g, p], 0.0).astype(BF16), kn_b[g, p])
    t_inv = _unit_lower_inverses([jnp.where(strict, kk[k] * decay[k], 0.0) for k in heads])
    res = {k: _dot(t.astype(BF16), rhs[k[0], k[1]]) for k, t in zip(heads, t_inv)}
    qkb = {k: (qk[k] * decay[k]).astype(BF16) for k in heads}

    u = {k: jnp.where(left, res[k + (0,)][:, :LANES], res[k + (1,)][:, :LANES]) for k in pairs}
    w = {k: jnp.where(left, res[k + (0,)][:, LANES:], res[k + (1,)][:, LANES:]) for k in pairs}
    s = {k: s_scr[k[0], k[1]] for k in pairs}
    sb = {k: s[k].astype(BF16) for k in pairs}
    v_new = {k: u[k] - _dot(w[k].astype(BF16), sb[k]) for k in pairs}
    vnb = {k: v_new[k].astype(BF16) for k in pairs}
    o = {k: _dot((qn[k] * egc_p[k]).astype(BF16), sb[k])
         + jnp.where(left, _dot(qkb[k + (0,)], vnb[k]), _dot(qkb[k + (1,)], vnb[k])) for k in pairs}
    upd = {k: _dot((kn[k] * erest_p[k]).T.astype(BF16), vnb[k]) for k in pairs}
    for (g, p) in pairs:
        lane = GDN_HEADS + 2 * p
        gt = jnp.where(row < GDN_DK, e_tot[g][:, lane:lane + 1], e_tot[g][:, lane + 1:lane + 2])
        s_scr[g, p] = s[g, p] * gt + jnp.where(blockdiag, upd[g, p], 0.0)
        ok = o[g, p]
        on = ok * lax.rsqrt(halfsum(ok * ok) * (1.0 / GDN_DV) + EPS) * ng_ref[...]
        z = gz_ref[g, :, p * LANES:(p + 1) * LANES]
        o_ref[g, :, p * LANES:(p + 1) * LANES] = (on * (z * _sigmoid(z))).astype(BF16)
    sfin_ref[...] = s_scr[...]


def _gdn(gqkv, gz, gba, hist, s0, lw, n_valid):
    B, L, _ = gqkv.shape
    G = GDN_ROWS
    row = lambda n: pl.BlockSpec((1, n), lambda b, l: (0, 0))
    return pl.pallas_call(
        functools.partial(_gdn_body, n_valid, L),
        grid=(B // G, L // CHUNK),
        in_specs=[pl.BlockSpec((G, CHUNK, 3 * GDN_DIM), lambda b, l: (b, l, 0)),
                  pl.BlockSpec((G, CHUNK, GDN_HEADS * GDN_DV), lambda b, l: (b, l, 0)),
                  pl.BlockSpec((G, CHUNK, LANES), lambda b, l: (b, l, 0)),
                  pl.BlockSpec((G, GDN_HIST, 3 * GDN_DIM), lambda b, l: (b, 0, 0)),
                  pl.BlockSpec((G, 2, LANES, LANES), lambda b, l: (b, 0, 0, 0)),
                  pl.BlockSpec((GDN_HIST, 3 * GDN_DIM), lambda b, l: (0, 0)),
                  row(LANES), row(LANES), row(LANES)],
        out_specs=[pl.BlockSpec((G, CHUNK, GDN_HEADS * GDN_DV), lambda b, l: (b, l, 0)),
                   pl.BlockSpec((G, 2, LANES, LANES), lambda b, l: (b, 0, 0, 0))],
        out_shape=[jax.ShapeDtypeStruct((B, L, GDN_HEADS * GDN_DV), BF16),
                   jax.ShapeDtypeStruct((B, 2, LANES, LANES), F32)],
        scratch_shapes=[pltpu.VMEM((G, CHUNK + GDN_HIST, 3 * GDN_DIM), F32),
                        pltpu.VMEM((G, 2, LANES, LANES), F32)],
        compiler_params=_params("parallel", "arbitrary"),
        name="gdn",
    )(gqkv, gz, gba, hist, s0, lw["gdn_conv_w"], lw["gdn_a_log"], lw["gdn_dt_bias"], lw["gdn_norm_g"])


def _merge_body(tm, x_ref, a_ref, c_ref, o_ref, p_ref, gmix_ref, wg_ref, wa_ref, wc_ref, wo_ref, wp_ref,
                wout_ref, gffn_ref, wrh_ref, wrm_ref, wrl_ref, br_ref,
                xnew_ref, xn2_ref, ri_ref, rw_ref, cnt_ref, carry):
    i = pl.program_id(0)

    @pl.when(i == 0)
    def _():
        carry[...] = jnp.zeros_like(carry)

    x = x_ref[...]
    xn = _rms(x, gmix_ref[...]).astype(BF16)
    h = None
    for bi, (b_ref, w_ref) in enumerate(((a_ref, wa_ref), (c_ref, wc_ref), (o_ref, wo_ref), (p_ref, wp_ref))):
        gate = _sigmoid(_dot(xn, wg_ref[:, bi * D_MODEL:(bi + 1) * D_MODEL]))
        term = gate * _dot(b_ref[...], w_ref[...])
        h = term if h is None else h + term
    x_new = x + _dot(h.astype(BF16), wout_ref[...])
    xnew_ref[...] = x_new
    xn2 = _rms(x_new, gffn_ref[...])
    xn2_ref[...] = xn2

    xh, xm, xl = _split3(xn2)
    logits = (_dot(xh, wrh_ref[...]) + (_dot(xh, wrm_ref[...]) + _dot(xm, wrh_ref[...]))
              + (_dot(xh, wrl_ref[...]) + _dot(xl, wrh_ref[...]) + _dot(xm, wrm_ref[...]))) + br_ref[...]
    lane = lax.broadcasted_iota(I32, (tm, LANES), 1).astype(F32)
    neg = -jnp.inf
    lg = jnp.where(lane < N_GROUPS, logits, neg)
    mg = jnp.max(lg, axis=-1, keepdims=True)
    p_grp = 1.0 / jnp.sum(jnp.exp(lg - mg), axis=-1, keepdims=True)
    g_sel = jnp.min(jnp.where(lg == mg, lane, float(LANES)), axis=-1, keepdims=True)
    lo = N_GROUPS + EXPERTS_PER_GROUP * g_sel
    le = jnp.where((lane >= lo) & (lane < lo + EXPERTS_PER_GROUP), logits, neg)
    m1 = jnp.max(le, axis=-1, keepdims=True)
    i1 = jnp.min(jnp.where(le == m1, lane, float(LANES)), axis=-1, keepdims=True)
    le2 = jnp.where(lane == i1, neg, le)
    m2 = jnp.max(le2, axis=-1, keepdims=True)
    i2 = jnp.min(jnp.where(le2 == m2, lane, float(LANES)), axis=-1, keepdims=True)
    e2 = jnp.exp(m2 - m1)
    w1 = p_grp / (1.0 + e2)
    w2 = p_grp * e2 / (1.0 + e2)
    e1 = i1 - N_GROUPS
    e2i = i2 - N_GROUPS

    oh1 = lane == e1
    oh2 = lane == e2i
    oh = jnp.where(oh1, 1.0, jnp.where(oh2, 1.0, 0.0))
    r = lax.broadcasted_iota(I32, (tm, tm), 0)
    c = lax.broadcasted_iota(I32, (tm, tm), 1)
    before = _dot(jnp.where(r > c, 1.0, 0.0).astype(BF16), oh.astype(BF16)) + carry[...]
    rank1 = jnp.sum(jnp.where(oh1, before, 0.0), axis=-1, keepdims=True)
    rank2 = jnp.sum(jnp.where(oh2, before, 0.0), axis=-1, keepdims=True)
    carry[...] = carry[...] + jnp.sum(oh, axis=0, keepdims=True)
    cnt_ref[...] = jnp.broadcast_to(carry[...], cnt_ref.shape)
    ri = jnp.where(lane == 0, e1, jnp.where(lane == 1, e2i, jnp.where(lane == 2, rank1,
                                                                      jnp.where(lane == 3, rank2, 0.0))))
    ri_ref[...] = ri.astype(I32)
    rw_ref[...] = jnp.where(lane == 0, w1, jnp.where(lane == 1, w2, 0.0))


def _merge(x, attn, c, o, pooled, lw):
    T = x.shape[0]
    tm = min(512, T)
    tile = lambda n: pl.BlockSpec((tm, n), lambda i: (i, 0))
    full = lambda a: pl.BlockSpec(a.shape, lambda i: (0,) * a.ndim)
    weights = [lw["norm_mix_g"], lw["w_gate"], lw["w_branch_attn"], lw["w_branch_conf"], lw["w_branch_gdn"],
               lw["w_branch_pool"], lw["w_out"], lw["norm_ffn_g"], lw["w_route_hi"], lw["w_route_mid"],
               lw["w_route_lo"], lw["b_route"]]
    return pl.pallas_call(
        functools.partial(_merge_body, tm),
        grid=(T // tm,),
        in_specs=[tile(D_MODEL), tile(SWA_Q), tile(CONF_DIM), tile(GDN_HEADS * GDN_DV), tile(POOL_DIM)]
                 + [full(w) for w in weights],
        out_specs=[tile(D_MODEL), tile(D_MODEL), tile(LANES), tile(LANES),
                   pl.BlockSpec((8, LANES), lambda i: (0, 0))],
        out_shape=[jax.ShapeDtypeStruct((T, D_MODEL), F32), jax.ShapeDtypeStruct((T, D_MODEL), F32),
                   jax.ShapeDtypeStruct((T, LANES), I32), jax.ShapeDtypeStruct((T, LANES), F32),
                   jax.ShapeDtypeStruct((8, LANES), F32)],
        scratch_shapes=[pltpu.VMEM((1, LANES), F32)],
        compiler_params=_params("arbitrary"),
        name="merge_route",
    )(x, attn, c, o, pooled, *weights)


def _row_copy(src, s_row, dst, d_row, sem):
    return pltpu.make_async_copy(src.at[pl.ds(s_row, 1), :], dst.at[pl.ds(d_row, 1), :], sem)


def _dispatch_body(tc, n_blocks, dest_ref, pend_ref, x_ref, xb_ref, zbuf, sem_z, sem_s):
    i = pl.program_id(0)

    def zero_block(start):
        return pltpu.make_async_copy(zbuf, xb_ref.at[pl.ds(pl.multiple_of(start, MOE_BLOCK), MOE_BLOCK), :], sem_z)

    def zero_copy(e):
        return zero_block(jnp.maximum(pend_ref[e] - MOE_BLOCK, 0))

    @pl.when(i == 0)
    def _():
        zbuf[...] = jnp.zeros_like(zbuf)
        n_used = pend_ref[N_EXPERTS - 1] // MOE_BLOCK

        @pl.loop(0, N_EXPERTS)
        def _(e):
            zero_copy(e).start()

        @pl.loop(n_used, n_blocks)
        def _(b):
            zero_block(b * MOE_BLOCK).start()

        @pl.loop(0, N_EXPERTS)
        def _(e):
            zero_copy(e).wait()

        @pl.loop(n_used, n_blocks)
        def _(b):
            zero_block(b * MOE_BLOCK).wait()

    def copies(r):
        slot = (i * tc + r) * 2
        return (_row_copy(x_ref, r, xb_ref, dest_ref[slot], sem_s),
                _row_copy(x_ref, r, xb_ref, dest_ref[slot + 1], sem_s))

    @pl.loop(0, tc)
    def _(r):
        for cp in copies(r):
            cp.start()

    @pl.loop(0, tc)
    def _(r):
        for cp in copies(r):
            cp.wait()


def _dispatch(xn2, dest_flat, pad_end, n_rows):
    T = xn2.shape[0]
    tc = min(512, T)
    return pl.pallas_call(
        functools.partial(_dispatch_body, tc, n_rows // MOE_BLOCK),
        grid_spec=pltpu.PrefetchScalarGridSpec(
            num_scalar_prefetch=2,
            grid=(T // tc,),
            in_specs=[pl.BlockSpec((tc, D_MODEL), lambda i, d, p: (i, 0))],
            out_specs=pl.BlockSpec(memory_space=pl.ANY),
            scratch_shapes=[pltpu.VMEM((MOE_BLOCK, D_MODEL), F32), pltpu.SemaphoreType.DMA(()),
                            pltpu.SemaphoreType.DMA(())]),
        out_shape=jax.ShapeDtypeStruct((n_rows, D_MODEL), F32),
        compiler_params=_params("arbitrary"),
        name="moe_dispatch",
    )(dest_flat, pad_end, xn2)


def _expert_body(be_ref, nu_ref, x_ref, wg_ref, wu_ref, wd_ref, y_ref, wg_b, wu_b, wd_b):
    i = pl.program_id(0)

    @pl.when((i == 0) | (be_ref[i] != be_ref[jnp.maximum(i - 1, 0)]))
    def _():
        wg_b[...] = wg_ref[...].astype(BF16)
        wu_b[...] = wu_ref[...].astype(BF16)
        wd_b[...] = wd_ref[...].astype(BF16)

    @pl.when(i < nu_ref[0])
    def _():
        x = x_ref[...].astype(BF16)
        a = _dot(x, wg_b[...])
        h = (a * _sigmoid(a)) * _dot(x, wu_b[...])
        y_ref[...] = _dot(h.astype(BF16), wd_b[...])

    @pl.when(i >= nu_ref[0])
    def _():
        y_ref[...] = jnp.zeros_like(y_ref)


def _experts(xb, block_expert, n_used, lw):
    n_rows = xb.shape[0]
    n_blocks = n_rows // MOE_BLOCK
    return pl.pallas_call(
        _expert_body,
        grid_spec=pltpu.PrefetchScalarGridSpec(
            num_scalar_prefetch=2,
            grid=(n_blocks,),
            in_specs=[pl.BlockSpec((MOE_BLOCK, D_MODEL), lambda i, be, nu: (jnp.minimum(i, nu[0] - 1), 0)),
                      pl.BlockSpec((None, D_MODEL, D_EXPERT), lambda i, be, nu: (be[i], 0, 0)),
                      pl.BlockSpec((None, D_MODEL, D_EXPERT), lambda i, be, nu: (be[i], 0, 0)),
                      pl.BlockSpec((None, D_EXPERT, D_MODEL), lambda i, be, nu: (be[i], 0, 0))],
            out_specs=pl.BlockSpec((MOE_BLOCK, D_MODEL), lambda i, be, nu: (i, 0)),
            scratch_shapes=[pltpu.VMEM((D_MODEL, D_EXPERT), BF16), pltpu.VMEM((D_MODEL, D_EXPERT), BF16),
                            pltpu.VMEM((D_EXPERT, D_MODEL), BF16)]),
        out_shape=jax.ShapeDtypeStruct((n_rows, D_MODEL), F32),
        compiler_params=_params("arbitrary"),
        name="moe_experts",
    )(block_expert, n_used, xb, lw["w_exp_gate"], lw["w_exp_up"], lw["w_exp_down"])


def _combine_body(tc, final, dest_ref, x_ref, w_ref, g_ref, yb_ref, out_ref, ybuf, sem):
    i = pl.program_id(0)

    def copies(step, buf, r):
        tok = step * tc + r
        return (_row_copy(yb_ref, dest_ref[2 * tok], ybuf.at[buf], r, sem.at[buf]),
                _row_copy(yb_ref, dest_ref[2 * tok + 1], ybuf.at[buf], tc + r, sem.at[buf]))

    def start_step(step, buf):
        @pl.loop(0, tc)
        def _(r):
            for cp in copies(step, buf, r):
                cp.start()

    @pl.when(i == 0)
    def _():
        start_step(i, 0)

    for buf in range(2):
        @pl.when(i % 2 == buf)
        def _():
            @pl.when(i + 1 < pl.num_programs(0))
            def _():
                start_step(i + 1, 1 - buf)

            @pl.loop(0, tc)
            def _(r):
                for cp in copies(i, buf, r):
                    cp.wait()

            w = w_ref[...]
            out = x_ref[...] + (ybuf[buf, 0:tc, :] * w[:, 0:1] + ybuf[buf, tc:2 * tc, :] * w[:, 1:2])
            if final:
                out = _rms(out, g_ref[...])
            out_ref[...] = out


def _combine(x_new, route_w, yb, dest_flat, g_final, final):
    T = x_new.shape[0]
    tc = min(256, T)
    return pl.pallas_call(
        functools.partial(_combine_body, tc, final),
        grid_spec=pltpu.PrefetchScalarGridSpec(
            num_scalar_prefetch=1,
            grid=(T // tc,),
            in_specs=[pl.BlockSpec((tc, D_MODEL), lambda i, d: (i, 0)),
                      pl.BlockSpec((tc, LANES), lambda i, d: (i, 0)),
                      pl.BlockSpec((1, D_MODEL), lambda i, d: (0, 0)),
                      pl.BlockSpec(memory_space=pl.ANY)],
            out_specs=pl.BlockSpec((tc, D_MODEL), lambda i, d: (i, 0)),
            scratch_shapes=[pltpu.VMEM((2, 2 * tc, D_MODEL), F32), pltpu.SemaphoreType.DMA((2,))]),
        out_shape=jax.ShapeDtypeStruct((T, D_MODEL), F32),
        compiler_params=_params("arbitrary"),
        name="moe_combine",
    )(dest_flat, x_new, route_w, g_final, yb)


def _moe(x_new, xn2, route_i, route_w, counts_f, lw, g_final, final):
    T = x_new.shape[0]
    counts = counts_f[0, :N_EXPERTS].astype(I32)
    padded = (counts + MOE_BLOCK - 1) // MOE_BLOCK * MOE_BLOCK
    pad_end = jnp.cumsum(padded)
    pad_start = pad_end - padded
    eid = route_i[:, 0:2]
    onehot = eid[:, :, None] == jnp.arange(N_EXPERTS, dtype=I32)[None, None, :]
    dest = route_i[:, 2:4] + jnp.sum(jnp.where(onehot, pad_start[None, None, :], 0), axis=-1)
    dest_flat = dest.reshape(-1).astype(I32)
    n_blocks = (2 * T + N_EXPERTS * (MOE_BLOCK - 1)) // MOE_BLOCK
    n_used = (pad_end[N_EXPERTS - 1:] // MOE_BLOCK).astype(I32)
    block_start = jnp.arange(n_blocks, dtype=I32) * MOE_BLOCK
    block_expert = jnp.minimum(jnp.sum(block_start[:, None] >= pad_end[None, :], axis=-1), N_EXPERTS - 1).astype(I32)
    xb = _dispatch(xn2, dest_flat, pad_end.astype(I32), n_blocks * MOE_BLOCK)
    yb = _experts(xb, block_expert, n_used, lw)
    return _combine(x_new, route_w, yb, dest_flat, g_final, final)


def _layer_weights(p, l):
    w_in = p["w_in"][l]
    q_w = w_in[:, :SWA_Q].reshape(D_MODEL, SWA_KV_HEADS, SWA_GROUP, HEAD_DIM).transpose(0, 2, 1, 3)
    off = SWA_Q + 2 * SWA_KV
    conf_w = w_in[:, off:off + 2 * CONF_DIM]
    off += 2 * CONF_DIM
    gqkv_w = w_in[:, off:off + 3 * GDN_DIM]
    off += 3 * GDN_DIM
    gz_w = w_in[:, off:off + GDN_HEADS * GDN_DV]
    off += GDN_HEADS * GDN_DV
    gba_w = jnp.pad(w_in[:, off:off + 2 * GDN_HEADS], ((0, 0), (0, LANES - 2 * GDN_HEADS)))
    off += 2 * GDN_HEADS
    pool_w_in = w_in[:, off:off + POOL_DIM]
    off += POOL_DIM
    w_small = jnp.concatenate([q_w.reshape(D_MODEL, SWA_Q), w_in[:, SWA_Q:SWA_Q + 2 * SWA_KV], conf_w, gqkv_w, gz_w,
                               pool_w_in, gba_w], axis=1).astype(BF16)
    pool_bd = jnp.zeros((POOL_DIM, POOL_DIM), F32)
    for gi in range(len(POOL_WINDOWS)):
        sl = slice(gi * POOL_GROUP_DIM, (gi + 1) * POOL_GROUP_DIM)
        pool_bd = pool_bd.at[sl, sl].set(p["pool_w"][l, gi])
    lane_pad = lambda v: jnp.pad(v, (GDN_HEADS, LANES - 2 * GDN_HEADS))[None, :]
    w_route = jnp.pad(jnp.concatenate([p["w_group"][l], p["w_router"][l]], axis=1),
                      ((0, 0), (0, LANES - N_GROUPS - N_EXPERTS)))
    r_hi = w_route.astype(BF16)
    r_mid = (w_route - r_hi.astype(F32)).astype(BF16)
    r_lo = (w_route - r_hi.astype(F32) - r_mid.astype(F32)).astype(BF16)
    return {
        "norm_mix_g": p["norm_mix_g"][l][None, :],
        "w_small": w_small,
        "w_gate": w_in[:, off:].astype(BF16),
        "swa_sinks": p["swa_sinks"][l],
        "conf_w": jnp.pad(p["conf_dw_w"][l], ((0, CONF_HIST - CONF_KERNEL), (0, 0))),
        "conf_b": p["conf_dw_b"][l][None, :],
        "conf_ln_g": p["conf_ln_g"][l][None, :],
        "conf_ln_b": p["conf_ln_b"][l][None, :],
        "gdn_conv_w": jnp.pad(p["gdn_conv_w"][l], ((0, GDN_HIST - GDN_CONV), (0, 0))),
        "gdn_a_log": lane_pad(p["gdn_a_log"][l]),
        "gdn_dt_bias": lane_pad(p["gdn_dt_bias"][l]),
        "gdn_norm_g": jnp.tile(p["gdn_norm_g"][l], 2)[None, :],
        "pool_w": pool_bd.astype(BF16),
        "pool_scale": p["pool_scale"][l][None, :],
        "w_branch_attn": p["w_branch_attn"][l].reshape(SWA_KV_HEADS, SWA_GROUP, HEAD_DIM, D_MODEL)
                         .transpose(1, 0, 2, 3).reshape(SWA_Q, D_MODEL).astype(BF16),
        "w_branch_conf": p["w_branch_conf"][l].astype(BF16),
        "w_branch_gdn": p["w_branch_gdn"][l].astype(BF16),
        "w_branch_pool": p["w_branch_pool"][l].astype(BF16),
        "w_out": p["w_out"][l].astype(BF16),
        "norm_ffn_g": p["norm_ffn_g"][l][None, :],
        "w_route_hi": r_hi, "w_route_mid": r_mid, "w_route_lo": r_lo,
        "b_route": jnp.pad(jnp.concatenate([p["b_group"][l], p["b_router"][l]]),
                           (0, LANES - N_GROUPS - N_EXPERTS))[None, :],
        "w_exp_gate": p["w_exp_gate"][l],
        "w_exp_up": p["w_exp_up"][l],
        "w_exp_down": p["w_exp_down"][l],
    }


def _pad_rows(a, n, front=False):
    extra = n - a.shape[1]
    return jnp.pad(a, ((0, 0), (extra, 0) if front else (0, extra), (0, 0)))


def _pair_state(s):
    z = jnp.zeros_like(s[:, 0::2])
    top = jnp.concatenate([s[:, 0::2], z], axis=-1)
    bot = jnp.concatenate([z, s[:, 1::2]], axis=-1)
    return jnp.concatenate([top, bot], axis=-2)


def _unpair_state(sp):
    even = sp[:, :, :GDN_DK, :GDN_DV]
    odd = sp[:, :, GDN_DK:, GDN_DV:]
    return jnp.stack([even, odd], axis=2).reshape(sp.shape[0], GDN_HEADS, GDN_DK, GDN_DV)


def _token_mixer(x, lw, state, start_pos):
    B, L, _ = x.shape
    T = B * L
    q, kv, conf_in, gqkv, gz, pool_in, gba = _proj(x.reshape(T, D_MODEL), lw["norm_mix_g"], lw["w_small"])
    seq = lambda a: a.reshape(B, L, a.shape[-1])
    q, kv, conf_in, gqkv, gz, pool_in, gba = map(seq, (q, kv, conf_in, gqkv, gz, pool_in, gba))
    has_past = state is not None
    if has_past:
        past_k, past_v, conf_buf, gdn_buf, gdn_s, pool_buf = state
        kv0 = jnp.concatenate([past_k.reshape(B, WINDOW, SWA_KV), past_v.reshape(B, WINDOW, SWA_KV)], axis=-1)
        lq, ll, lg = WINDOW, 8, CHUNK
    else:
        conf_buf = jnp.zeros((B, CONF_KERNEL - 1, CONF_DIM), F32)
        gdn_buf = jnp.zeros((B, GDN_CONV - 1, 3 * GDN_DIM), F32)
        gdn_s = jnp.zeros((B, GDN_HEADS, GDN_DK, GDN_DV), F32)
        pool_buf = jnp.zeros((B, POOL_MAX - 1, POOL_DIM), F32)
        kv0 = jnp.zeros((B, WINDOW, 2 * SWA_KV), F32)
        lq = ll = lg = L
    attn = _attn(_pad_rows(q, lq), _pad_rows(kv, lq), kv0, lw["swa_sinks"], has_past)[:, :L]
    c, pooled, cu_last = _local_mixers(_pad_rows(conf_in, ll), _pad_rows(pool_in, ll),
                                       _pad_rows(conf_buf, CONF_HIST, front=True),
                                       _pad_rows(pool_buf, POOL_HIST, front=True), lw, start_pos)
    o, s_fin = _gdn(_pad_rows(gqkv, lg), _pad_rows(gz, lg), _pad_rows(gba, lg),
                    _pad_rows(gdn_buf, GDN_HIST, front=True), _pair_state(gdn_s), lw, L)
    if has_past:
        new_kv = jnp.concatenate([kv0[:, L:], kv], axis=1)
        n_hist = cu_last.shape[1]
        new_conf = jnp.concatenate([conf_buf[:, L:], cu_last[:, n_hist - ll:n_hist - ll + L]], axis=1)
        new_gdn_buf = jnp.concatenate([gdn_buf, gqkv], axis=1)[:, -(GDN_CONV - 1):]
        new_pool = jnp.concatenate([pool_buf[:, L:], pool_in], axis=1)
    else:
        new_kv = kv[:, -WINDOW:]
        new_conf = cu_last[:, -(CONF_KERNEL - 1):]
        new_gdn_buf = gqkv[:, -(GDN_CONV - 1):]
        new_pool = pool_in[:, -(POOL_MAX - 1):]
    new_k = new_kv[..., :SWA_KV].reshape(B, WINDOW, SWA_KV_HEADS, HEAD_DIM)
    new_v = new_kv[..., SWA_KV:].reshape(B, WINDOW, SWA_KV_HEADS, HEAD_DIM)
    flat = lambda a: a[:, :L].reshape(T, a.shape[-1])
    branches = (flat(attn), flat(c), flat(o), flat(pooled))
    return branches, (new_k, new_v, new_conf, new_gdn_buf, _unpair_state(s_fin), new_pool)


def _trunk(x, past, start_pos, layers, g_final):
    B, L, _ = x.shape
    collected = [[] for _ in range(6)]
    xf = x.reshape(B * L, D_MODEL)
    for l, lw in enumerate(layers):
        state = None if past is None else tuple(p[l] for p in past)
        branches, new = _token_mixer(xf.reshape(B, L, D_MODEL), lw, state, start_pos)
        x_new, xn2, route_i, route_w, counts = _merge(xf, *branches, lw)
        xf = _moe(x_new, xn2, route_i, route_w, counts, lw, g_final, l == len(layers) - 1)
        for lst, arr in zip(collected, new):
            lst.append(arr)
    return xf.reshape(B, L, D_MODEL), [jnp.stack(lst, axis=0) for lst in collected]


def kernel(x_prompt, x_sample, cache_swa_k, cache_swa_v, state_conf_conv, state_gdn_conv, state_gdn, state_pool, norm_mix_g, w_in, swa_sinks, conf_dw_w, conf_dw_b, conf_ln_g, conf_ln_b, gdn_conv_w, gdn_a_log, gdn_dt_bias, gdn_norm_g, pool_w, pool_scale, w_branch_attn, w_branch_conf, w_branch_gdn, w_branch_pool, w_out, norm_ffn_g, w_group, b_group, w_router, b_router, w_exp_gate, w_exp_up, w_exp_down, final_norm_g):
    params = dict(norm_mix_g=norm_mix_g, w_in=w_in, swa_sinks=swa_sinks, conf_dw_w=conf_dw_w,
                  conf_dw_b=conf_dw_b, conf_ln_g=conf_ln_g, conf_ln_b=conf_ln_b, gdn_conv_w=gdn_conv_w,
                  gdn_a_log=gdn_a_log, gdn_dt_bias=gdn_dt_bias, gdn_norm_g=gdn_norm_g, pool_w=pool_w,
                  pool_scale=pool_scale, w_branch_attn=w_branch_attn, w_branch_conf=w_branch_conf,
                  w_branch_gdn=w_branch_gdn, w_branch_pool=w_branch_pool, w_out=w_out, norm_ffn_g=norm_ffn_g,
                  w_group=w_group, b_group=b_group, w_router=w_router, b_router=b_router,
                  w_exp_gate=w_exp_gate, w_exp_up=w_exp_up, w_exp_down=w_exp_down)
    layers = [_layer_weights(params, l) for l in range(w_in.shape[0])]
    g_final = final_norm_g[None, :]
    yp, (pk, pv, pc, pgc, pgs, pp) = _trunk(x_prompt, None, 0, layers, g_final)
    past = (cache_swa_k, cache_swa_v, state_conf_conv, state_gdn_conv, state_gdn, state_pool)
    ys, (sk, sv, sc, sgc, sgs, sp) = _trunk(x_sample, past, PAST_LEN, layers, g_final)
    return (yp, ys, pk, pv, pc, pgc, pgs, pp, sk, sv, sc, sgc, sgs, sp)
```

```python
import functools

import jax
import jax.numpy as jnp
from jax import lax
from jax.experimental import pallas as pl
from jax.experimental.pallas import tpu as pltpu

F32 = jnp.float32
BF16 = jnp.bfloat16
I32 = jnp.int32

D_MODEL = 1024
DEPTH = 2
PAST_LEN = 16384
HEAD_DIM = 64
SWA_HEADS = 6
SWA_KV_HEADS = 2
SWA_GROUP = SWA_HEADS // SWA_KV_HEADS
SWA_Q = SWA_HEADS * HEAD_DIM
SWA_KV = SWA_KV_HEADS * HEAD_DIM
WINDOW = 128
ATTN_SCALE = HEAD_DIM ** -0.5
CONF_DIM = D_MODEL // 4
CONF_KERNEL = 31
GDN_HEADS = 4
GDN_DK = 64
GDN_DV = 64
GDN_DIM = GDN_HEADS * GDN_DK
GDN_CONV = 4
POOL_WINDOWS = (2, 4, 8, 16)
POOL_DIM = D_MODEL // 4
POOL_GROUP_DIM = POOL_DIM // len(POOL_WINDOWS)
POOL_MAX = max(POOL_WINDOWS)
N_BRANCHES = 4
N_GROUPS = 4
EXPERTS_PER_GROUP = 8
N_EXPERTS = N_GROUPS * EXPERTS_PER_GROUP
D_EXPERT = D_MODEL // 2
MOE_BLOCK = 256
EPS = 1e-6

LANES = 128
ROW_SUB = D_MODEL // LANES
CHUNK = 128
CONF_HIST = 32
POOL_HIST = 16
GDN_HIST = 8
GDN_ROWS = 4
ATTN_STEP_ROWS = 8
STEP_PAD = 8
VMEM_LIMIT = 56 * 1024 * 1024

PROJ_SPLITS = (SWA_Q, 2 * SWA_KV, 2 * CONF_DIM, 3 * GDN_DIM, GDN_HEADS * GDN_DV, POOL_DIM, LANES)
W_IN_OFFS = (0, SWA_Q, SWA_Q + SWA_KV, SWA_Q + 2 * SWA_KV)


def _params(*sem):
    return pltpu.CompilerParams(dimension_semantics=sem, vmem_limit_bytes=VMEM_LIMIT)


def _rms(x, g):
    return x * lax.rsqrt(jnp.mean(x * x, axis=-1, keepdims=True) + EPS) * g


def _sigmoid(x):
    return 1.0 / (1.0 + jnp.exp(-x))


def _dot(a, b):
    return jnp.dot(a, b, preferred_element_type=F32)


def _dot_nt(a, b):
    return lax.dot_general(a, b, (((1,), (1,)), ((), ())), preferred_element_type=F32)


def _split3(x):
    hi = x.astype(BF16)
    r = x - hi.astype(F32)
    mid = r.astype(BF16)
    lo = (r - mid.astype(F32)).astype(BF16)
    return hi, mid, lo


def _dot_exact_lhs(a_bf16, b):
    hi, mid, lo = _split3(b)
    return _dot(a_bf16, hi) + _dot(a_bf16, mid) + _dot(a_bf16, lo)


def _proj_body(x_ref, g_ref, w_ref, *out_refs):
    xn = _rms(x_ref[...], g_ref[...]).astype(BF16)
    off = 0
    for o_ref, n in zip(out_refs, PROJ_SPLITS):
        o_ref[...] = _dot(xn, w_ref[:, off:off + n])
        off += n


def _proj(x, g, w_small):
    T = x.shape[0]
    tm = min(512, T)
    n_small = sum(PROJ_SPLITS)
    return pl.pallas_call(
        _proj_body,
        grid=(T // tm,),
        in_specs=[pl.BlockSpec((tm, D_MODEL), lambda i: (i, 0)),
                  pl.BlockSpec((1, D_MODEL), lambda i: (0, 0)),
                  pl.BlockSpec((D_MODEL, n_small), lambda i: (0, 0))],
        out_specs=[pl.BlockSpec((tm, n), lambda i: (i, 0)) for n in PROJ_SPLITS],
        out_shape=[jax.ShapeDtypeStruct((T, n), F32) for n in PROJ_SPLITS],
        compiler_params=_params("parallel"),
        name="proj",
    )(x, g, w_small)


def _attn_body(sink_ref, q_ref, kv_ref, kvp_ref, o_ref):
    i = pl.program_id(1)
    q = q_ref[...] * ATTN_SCALE
    kvc = kv_ref[...]
    kvp = kvp_ref[...]
    k2 = jnp.concatenate([kvp[:, :LANES], kvc[:, :LANES]], axis=0).astype(BF16)
    v2 = jnp.concatenate([kvp[:, LANES:], kvc[:, LANES:]], axis=0).astype(BF16)
    lane = lax.broadcasted_iota(I32, (WINDOW, LANES), 1)
    row = lax.broadcasted_iota(I32, (WINDOW, 2 * WINDOW), 0)
    col = lax.broadcasted_iota(I32, (WINDOW, 2 * WINDOW), 1)
    valid = (col > row) & (col <= row + WINDOW) & ((col >= WINDOW) | (i > 0))
    for g in range(SWA_GROUP):
        qg = q[:, g * LANES:(g + 1) * LANES]
        og = jnp.zeros((WINDOW, LANES), F32)
        for j in range(SWA_KV_HEADS):
            half = (lane >= HEAD_DIM) if j == 1 else (lane < HEAD_DIM)
            qj = jnp.where(half, qg, 0.0).astype(BF16)
            s = jnp.where(valid, _dot_nt(qj, k2), -jnp.inf)
            sink = sink_ref[j * SWA_GROUP + g]
            m = jnp.maximum(jnp.max(s, axis=-1, keepdims=True), sink)
            p = jnp.exp(s - m)
            denom = jnp.sum(p, axis=-1, keepdims=True) + jnp.exp(sink - m)
            pv = _dot(p.astype(BF16), v2)
            og = jnp.where(half, pv / denom, og)
        o_ref[:, g * LANES:(g + 1) * LANES] = og.astype(BF16)


def _attn(q, kv, sinks):
    B, L, _ = q.shape
    nb = L // WINDOW
    return pl.pallas_call(
        _attn_body,
        grid=(B, nb),
        in_specs=[pl.BlockSpec(memory_space=pltpu.SMEM),
                  pl.BlockSpec((None, WINDOW, SWA_Q), lambda b, i: (b, i, 0)),
                  pl.BlockSpec((None, WINDOW, 2 * SWA_KV), lambda b, i: (b, i, 0)),
                  pl.BlockSpec((None, WINDOW, 2 * SWA_KV), lambda b, i: (b, jnp.maximum(i - 1, 0), 0))],
        out_specs=pl.BlockSpec((None, WINDOW, SWA_Q), lambda b, i: (b, i, 0)),
        out_shape=jax.ShapeDtypeStruct((B, L, SWA_Q), BF16),
        compiler_params=_params("parallel", "arbitrary"),
        name="swa",
    )(sinks, q, kv, kv)


def _attn_step_body(sink_ref, q_ref, kv_ref, kv0_ref, o_ref):
    n_seq, tq, _ = q_ref.shape
    row = lax.broadcasted_iota(I32, (tq, LANES), 0)
    col = lax.broadcasted_iota(I32, (tq, LANES), 1)
    in_cache = col > row
    causal = col <= row
    zpad = jnp.zeros((WINDOW - tq, LANES), F32)
    for b in range(n_seq):
        q = q_ref[b] * ATTN_SCALE
        kvp = kv0_ref[b]
        kvc = kv_ref[b]
        kp = kvp[:, :LANES].astype(BF16)
        vp = kvp[:, LANES:].astype(BF16)
        kc = jnp.concatenate([kvc[:, :LANES], zpad], axis=0).astype(BF16)
        vc = jnp.concatenate([kvc[:, LANES:], zpad], axis=0).astype(BF16)
        for g in range(SWA_GROUP):
            qg = q[:, g * LANES:(g + 1) * LANES]
            og = jnp.zeros((tq, LANES), F32)
            for j in range(SWA_KV_HEADS):
                half = (col >= HEAD_DIM) if j == 1 else (col < HEAD_DIM)
                qj = jnp.where(half, qg, 0.0).astype(BF16)
                sp = jnp.where(in_cache, _dot_nt(qj, kp), -jnp.inf)
                sc = jnp.where(causal, _dot_nt(qj, kc), -jnp.inf)
                sink = sink_ref[j * SWA_GROUP + g]
                m = jnp.maximum(jnp.maximum(jnp.max(sp, axis=-1, keepdims=True),
                                            jnp.max(sc, axis=-1, keepdims=True)), sink)
                pp = jnp.exp(sp - m)
                pc = jnp.exp(sc - m)
                denom = (jnp.sum(pp, axis=-1, keepdims=True) + jnp.sum(pc, axis=-1, keepdims=True)
                         + jnp.exp(sink - m))
                pv = _dot(pp.astype(BF16), vp) + _dot(pc.astype(BF16), vc)
                og = jnp.where(half, pv / denom, og)
            o_ref[b, :, g * LANES:(g + 1) * LANES] = og.astype(BF16)


def _attn_step(q, kv, kv0, sinks):
    B, tq, _ = q.shape
    G = ATTN_STEP_ROWS
    return pl.pallas_call(
        _attn_step_body,
        grid=(B // G,),
        in_specs=[pl.BlockSpec(memory_space=pltpu.SMEM),
                  pl.BlockSpec((G, tq, SWA_Q), lambda b: (b, 0, 0)),
                  pl.BlockSpec((G, tq, 2 * SWA_KV), lambda b: (b, 0, 0)),
                  pl.BlockSpec((G, WINDOW, 2 * SWA_KV), lambda b: (b, 0, 0))],
        out_specs=pl.BlockSpec((G, tq, SWA_Q), lambda b: (b, 0, 0)),
        out_shape=jax.ShapeDtypeStruct((B, tq, SWA_Q), BF16),
        compiler_params=_params("parallel"),
        name="swa_step",
    )(sinks, q, kv, kv0)


def _local_body(tl, start_pos, conf_ref, pool_ref, chist_ref, phist_ref, cw_ref, cb_ref, lng_ref, lnb_ref,
                pw_ref, ps_ref, c_ref, pooled_ref, culast_ref, cbuf, pbuf):
    for g in range(conf_ref.shape[0]):
        _local_row(tl, start_pos, conf_ref.at[g], pool_ref.at[g], chist_ref.at[g], phist_ref.at[g], cw_ref, cb_ref,
                   lng_ref, lnb_ref, pw_ref, ps_ref, c_ref.at[g], pooled_ref.at[g], culast_ref.at[g], cbuf.at[g],
                   pbuf.at[g])


def _local_row(tl, start_pos, conf_ref, pool_ref, chist_ref, phist_ref, cw_ref, cb_ref, lng_ref, lnb_ref,
               pw_ref, ps_ref, c_ref, pooled_ref, culast_ref, cbuf, pbuf):
    l = pl.program_id(1)

    @pl.when(l == 0)
    def _():
        cbuf[0:CONF_HIST, :] = chist_ref[...]
        pbuf[0:POOL_HIST, :] = phist_ref[...]

    ci = conf_ref[...]
    cbuf[CONF_HIST:CONF_HIST + tl, :] = ci[:, :CONF_DIM] * _sigmoid(ci[:, CONF_DIM:])
    base = CONF_HIST - (CONF_KERNEL - 1)
    acc = cw_ref[0:1, :] * cbuf[base:base + tl, :]
    for j in range(1, CONF_KERNEL):
        acc = acc + cw_ref[j:j + 1, :] * cbuf[base + j:base + j + tl, :]
    y = acc + cb_ref[...]
    yc = y - jnp.mean(y, axis=-1, keepdims=True)
    yn = yc * lax.rsqrt(jnp.mean(yc * yc, axis=-1, keepdims=True) + EPS) * lng_ref[...] + lnb_ref[...]
    c_ref[...] = (yn * _sigmoid(yn)).astype(BF16)
    tail = cbuf[tl:tl + CONF_HIST, :]
    culast_ref[...] = tail
    cbuf[0:CONF_HIST, :] = tail

    u = pool_ref[...]
    pbuf[POOL_HIST:POOL_HIST + tl, :] = u
    lane = lax.broadcasted_iota(I32, (tl, POOL_DIM), 1)
    n_avail = (start_pos + 1 + l * tl + lax.broadcasted_iota(I32, (tl, POOL_DIM), 0)).astype(F32)
    wsum = u
    mean = None
    shift = 1
    for gi, wlen in enumerate(POOL_WINDOWS):
        while shift < wlen:
            wsum = wsum + pbuf[POOL_HIST - shift:POOL_HIST - shift + tl, :]
            shift += 1
        m = wsum / jnp.minimum(n_avail, float(wlen))
        mean = m if mean is None else jnp.where(lane >= gi * POOL_GROUP_DIM, m, mean)
    mixed = (mean - u).astype(BF16)
    pooled_ref[...] = (_dot(mixed, pw_ref[...]) * ps_ref[...]).astype(BF16)
    pbuf[0:POOL_HIST, :] = pbuf[tl:tl + POOL_HIST, :]


def _local_mixers(conf_in, pool_in, conf_hist, pool_hist, lw, start_pos):
    B, L, _ = conf_in.shape
    tl = min(512, L)
    G = 1 if L > STEP_PAD else ATTN_STEP_ROWS
    row = lambda n: pl.BlockSpec((1, n), lambda b, l: (0, 0))
    return pl.pallas_call(
        functools.partial(_local_body, tl, start_pos),
        grid=(B // G, L // tl),
        in_specs=[pl.BlockSpec((G, tl, 2 * CONF_DIM), lambda b, l: (b, l, 0)),
                  pl.BlockSpec((G, tl, POOL_DIM), lambda b, l: (b, l, 0)),
                  pl.BlockSpec((G, CONF_HIST, CONF_DIM), lambda b, l: (b, 0, 0)),
                  pl.BlockSpec((G, POOL_HIST, POOL_DIM), lambda b, l: (b, 0, 0)),
                  pl.BlockSpec((CONF_HIST, CONF_DIM), lambda b, l: (0, 0)),
                  row(CONF_DIM), row(CONF_DIM), row(CONF_DIM),
                  pl.BlockSpec((POOL_DIM, POOL_DIM), lambda b, l: (0, 0)),
                  row(POOL_DIM)],
        out_specs=[pl.BlockSpec((G, tl, CONF_DIM), lambda b, l: (b, l, 0)),
                   pl.BlockSpec((G, tl, POOL_DIM), lambda b, l: (b, l, 0)),
                   pl.BlockSpec((G, CONF_HIST, CONF_DIM), lambda b, l: (b, 0, 0))],
        out_shape=[jax.ShapeDtypeStruct((B, L, CONF_DIM), BF16),
                   jax.ShapeDtypeStruct((B, L, POOL_DIM), BF16),
                   jax.ShapeDtypeStruct((B, CONF_HIST, CONF_DIM), F32)],
        scratch_shapes=[pltpu.VMEM((G, tl + CONF_HIST, CONF_DIM), F32),
                        pltpu.VMEM((G, tl + POOL_HIST, POOL_DIM), F32)],
        compiler_params=_params("parallel", "arbitrary"),
        name="conv_pool",
    )(conf_in, pool_in, conf_hist, pool_hist, lw["conf_w"], lw["conf_b"], lw["conf_ln_g"], lw["conf_ln_b"],
      lw["pool_w"], lw["pool_scale"])


def _unit_lower_inverses(mats):
    row = lax.broadcasted_iota(I32, (CHUNK, CHUNK), 0)
    col = lax.broadcasted_iota(I32, (CHUNK, CHUNK), 1)
    eye = jnp.where(row == col, 1.0, 0.0)
    ps = [-a for a in mats]
    ts = [eye + n for n in ps]
    for _ in range(CHUNK.bit_length() - 2):
        pbs = [p.astype(BF16) for p in ps]
        ps = [_dot(pb, pb) for pb in pbs]
        ts = [t + _dot(t.astype(BF16), p.astype(BF16)) for t, p in zip(ts, ps)]
    splits = [(_split3(t), _split3(eye + a)) for t, a in zip(ts, mats)]
    resids = [eye - (_dot(mh, th) + (_dot(mh, tm) + _dot(mm, th))) for (th, tm, _), (mh, mm, _) in splits]
    return [t + _dot(sp[0][0], r.astype(BF16)) for t, sp, r in zip(ts, splits, resids)]


def _gdn_body(n_valid, seq_len, gqkv_ref, gz_ref, gba_ref, hist_ref, s0_ref, cw_ref, alog_ref, dtb_ref, ng_ref,
              o_ref, sfin_ref, cbuf, s_scr):
    l = pl.program_id(1)
    rows = range(GDN_ROWS)
    pairs = [(g, p) for g in rows for p in range(GDN_HEADS // 2)]
    heads = [(g, p, hh) for (g, p) in pairs for hh in range(2)]

    @pl.when(l == 0)
    def _():
        cbuf[:, 0:GDN_HIST, :] = hist_ref[...]
        s_scr[...] = s0_ref[...]

    row = lax.broadcasted_iota(I32, (CHUNK, LANES), 0)
    col = lax.broadcasted_iota(I32, (CHUNK, LANES), 1)
    left = col < GDN_DK
    right = jnp.logical_not(left)
    lower = row >= col
    strict = row > col
    blockdiag = (row < GDN_DK) == left

    def halfsum(x):
        sl = jnp.sum(jnp.where(left, x, 0.0), axis=-1, keepdims=True)
        sr = jnp.sum(jnp.where(left, 0.0, x), axis=-1, keepdims=True)
        return jnp.where(left, sl, sr)

    def pair_cols(arr, lane0):
        return jnp.where(left, arr[:, lane0:lane0 + 1], arr[:, lane0 + 1:lane0 + 2])

    base = GDN_HIST - (GDN_CONV - 1)
    qkv = {}
    for g in rows:
        cbuf[g, GDN_HIST:GDN_HIST + CHUNK, :] = gqkv_ref[g]
        acc = cw_ref[0:1, :] * cbuf[g, base:base + CHUNK, :]
        for j in range(1, GDN_CONV):
            acc = acc + cw_ref[j:j + 1, :] * cbuf[g, base + j:base + j + CHUNK, :]
        cbuf[g, 0:GDN_HIST, :] = cbuf[g, CHUNK:CHUNK + GDN_HIST, :]
        qkv[g] = acc * _sigmoid(acc)

    ltri = jnp.where(lower, 1.0, 0.0).astype(BF16)
    beta_all, g_parts = {}, {}
    for g in rows:
        gba = gba_ref[g]
        beta = _sigmoid(gba)
        ga = gba + dtb_ref[...]
        softplus = jnp.maximum(ga, 0.0) + jnp.log(1.0 + jnp.exp(-jnp.abs(ga)))
        gl = -jnp.exp(alog_ref[...]) * softplus
        if n_valid < seq_len:
            live = (l * CHUNK + row) < n_valid
            beta = jnp.where(live, beta, 0.0)
            gl = jnp.where(live, gl, 0.0)
        beta_all[g] = beta
        g_parts[g] = _split3(gl)
    gc_all = {g: _dot(ltri, g_parts[g][0]) + _dot(ltri, g_parts[g][1]) + _dot(ltri, g_parts[g][2]) for g in rows}
    gc_t = {g: gc_all[g].T for g in rows}
    g_last = {g: gc_all[g][CHUNK - 1:CHUNK, :] for g in rows}
    e_gc = {g: jnp.exp(gc_all[g]) for g in rows}
    e_rest = {g: jnp.exp(g_last[g] - gc_all[g]) for g in rows}
    e_tot = {g: jnp.exp(g_last[g]) for g in rows}

    qn, kn, kn_b, kb, rhs, egc_p, erest_p = {}, {}, {}, {}, {}, {}, {}
    for (g, p) in pairs:
        h0 = 2 * p
        qp = qkv[g][:, p * LANES:(p + 1) * LANES]
        kp = qkv[g][:, GDN_DIM + p * LANES:GDN_DIM + (p + 1) * LANES]
        vp = qkv[g][:, 2 * GDN_DIM + p * LANES:2 * GDN_DIM + (p + 1) * LANES]
        qn[g, p] = qp * lax.rsqrt(halfsum(qp * qp) + EPS) * (GDN_DK ** -0.5)
        kn[g, p] = kp * lax.rsqrt(halfsum(kp * kp) + EPS)
        beta_p = pair_cols(beta_all[g], h0)
        egc_p[g, p] = pair_cols(e_gc[g], GDN_HEADS + h0)
        erest_p[g, p] = pair_cols(e_rest[g], GDN_HEADS + h0)
        kb[g, p] = kn[g, p] * beta_p
        rhs[g, p] = jnp.concatenate([vp * beta_p, kb[g, p] * egc_p[g, p]], axis=1).astype(BF16)
        kn_b[g, p] = kn[g, p].astype(BF16)

    decay, kk, qk = {}, {}, {}
    for (g, p, hh) in heads:
        lane = GDN_HEADS + 2 * p + hh
        hm = left if hh == 0 else right
        diff = gc_all[g][:, lane:lane + 1] - gc_t[g][lane:lane + 1, :]
        decay[g, p, hh] = jnp.where(lower, jnp.exp(jnp.where(lower, diff, 0.0)), 0.0)
        kk[g, p, hh] = _dot_nt(jnp.where(hm, kb[g, p], 0.0).astype(BF16), kn_b[g, p])
        qk[g, p, hh] = _dot_nt(jnp.where(hm, qn[g, p], 0.0).astype(BF16), kn_b[g, p])
    t_inv = _unit_lower_inverses([jnp.where(strict, kk[k] * decay[k], 0.0) for k in heads])
    res = {k: _dot(t.astype(BF16), rhs[k[0], k[1]]) for k, t in zip(heads, t_inv)}
    qkb = {k: (qk[k] * decay[k]).astype(BF16) for k in heads}

    u = {k: jnp.where(left, res[k + (0,)][:, :LANES], res[k + (1,)][:, :LANES]) for k in pairs}
    w = {k: jnp.where(left, res[k + (0,)][:, LANES:], res[k + (1,)][:, LANES:]) for k in pairs}
    s = {k: s_scr[k[0], k[1]] for k in pairs}
    sb = {k: s[k].astype(BF16) for k in pairs}
    v_new = {k: u[k] - _dot(w[k].astype(BF16), sb[k]) for k in pairs}
    vnb = {k: v_new[k].astype(BF16) for k in pairs}
    o = {k: _dot((qn[k] * egc_p[k]).astype(BF16), sb[k])
         + jnp.where(left, _dot(qkb[k + (0,)], vnb[k]), _dot(qkb[k + (1,)], vnb[k])) for k in pairs}
    upd = {k: _dot((kn[k] * erest_p[k]).T.astype(BF16), vnb[k]) for k in pairs}
    for (g, p) in pairs:
        lane = GDN_HEADS + 2 * p
        gt = jnp.where(row < GDN_DK, e_tot[g][:, lane:lane + 1], e_tot[g][:, lane + 1:lane + 2])
        s_scr[g, p] = s[g, p] * gt + jnp.where(blockdiag, upd[g, p], 0.0)
        ok = o[g, p]
        on = ok * lax.rsqrt(halfsum(ok * ok) * (1.0 / GDN_DV) + EPS) * ng_ref[...]
        z = gz_ref[g, :, p * LANES:(p + 1) * LANES]
        o_ref[g, :, p * LANES:(p + 1) * LANES] = (on * (z * _sigmoid(z))).astype(BF16)
    sfin_ref[...] = s_scr[...]


def _gdn(gqkv, gz, gba, hist, s0, lw, n_valid):
    B, L, _ = gqkv.shape
    G = GDN_ROWS
    row = lambda n: pl.BlockSpec((1, n), lambda b, l: (0, 0))
    return pl.pallas_call(
        functools.partial(_gdn_body, n_valid, L),
        grid=(B // G, L // CHUNK),
        in_specs=[pl.BlockSpec((G, CHUNK, 3 * GDN_DIM), lambda b, l: (b, l, 0)),
                  pl.BlockSpec((G, CHUNK, GDN_HEADS * GDN_DV), lambda b, l: (b, l, 0)),
                  pl.BlockSpec((G, CHUNK, LANES), lambda b, l: (b, l, 0)),
                  pl.BlockSpec((G, GDN_HIST, 3 * GDN_DIM), lambda b, l: (b, 0, 0)),
                  pl.BlockSpec((G, 2, LANES, LANES), lambda b, l: (b, 0, 0, 0)),
                  pl.BlockSpec((GDN_HIST, 3 * GDN_DIM), lambda b, l: (0, 0)),
                  row(LANES), row(LANES), row(LANES)],
        out_specs=[pl.BlockSpec((G, CHUNK, GDN_HEADS * GDN_DV), lambda b, l: (b, l, 0)),
                   pl.BlockSpec((G, 2, LANES, LANES), lambda b, l: (b, 0, 0, 0))],
        out_shape=[jax.ShapeDtypeStruct((B, L, GDN_HEADS * GDN_DV), BF16),
                   jax.ShapeDtypeStruct((B, 2, LANES, LANES), F32)],
        scratch_shapes=[pltpu.VMEM((G, CHUNK + GDN_HIST, 3 * GDN_DIM), F32),
                        pltpu.VMEM((G, 2, LANES, LANES), F32)],
        compiler_params=_params("parallel", "arbitrary"),
        name="gdn",
    )(gqkv, gz, gba, hist, s0, lw["gdn_conv_w"], lw["gdn_a_log"], lw["gdn_dt_bias"], lw["gdn_norm_g"])


def _merge_body(tm, x_ref, a_ref, c_ref, o_ref, p_ref, gmix_ref, wg_ref, wa_ref, wc_ref, wo_ref, wp_ref,
                wout_ref, gffn_ref, wrh_ref, wrm_ref, wrl_ref, br_ref,
                xnew_ref, xn2_ref, ri_ref, rw_ref, cnt_ref, carry):
    i = pl.program_id(0)

    @pl.when(i == 0)
    def _():
        carry[...] = jnp.zeros_like(carry)

    x = x_ref[...]
    xn = _rms(x, gmix_ref[...]).astype(BF16)
    h = None
    for bi, (b_ref, w_ref) in enumerate(((a_ref, wa_ref), (c_ref, wc_ref), (o_ref, wo_ref), (p_ref, wp_ref))):
        gate = _sigmoid(_dot(xn, wg_ref[:, bi * D_MODEL:(bi + 1) * D_MODEL]))
        term = gate * _dot(b_ref[...], w_ref[...])
        h = term if h is None else h + term
    x_new = x + _dot(h.astype(BF16), wout_ref[...])
    xnew_ref[...] = x_new
    xn2 = _rms(x_new, gffn_ref[...])
    _store_row_tiles(xn2_ref, xn2)

    xh, xm, xl = _split3(xn2)
    logits = (_dot(xh, wrh_ref[...]) + (_dot(xh, wrm_ref[...]) + _dot(xm, wrh_ref[...]))
              + (_dot(xh, wrl_ref[...]) + _dot(xl, wrh_ref[...]) + _dot(xm, wrm_ref[...]))) + br_ref[...]
    lane = lax.broadcasted_iota(I32, (tm, LANES), 1).astype(F32)
    neg = -jnp.inf
    lg = jnp.where(lane < N_GROUPS, logits, neg)
    mg = jnp.max(lg, axis=-1, keepdims=True)
    p_grp = 1.0 / jnp.sum(jnp.exp(lg - mg), axis=-1, keepdims=True)
    g_sel = jnp.min(jnp.where(lg == mg, lane, float(LANES)), axis=-1, keepdims=True)
    lo = N_GROUPS + EXPERTS_PER_GROUP * g_sel
    le = jnp.where((lane >= lo) & (lane < lo + EXPERTS_PER_GROUP), logits, neg)
    m1 = jnp.max(le, axis=-1, keepdims=True)
    i1 = jnp.min(jnp.where(le == m1, lane, float(LANES)), axis=-1, keepdims=True)
    le2 = jnp.where(lane == i1, neg, le)
    m2 = jnp.max(le2, axis=-1, keepdims=True)
    i2 = jnp.min(jnp.where(le2 == m2, lane, float(LANES)), axis=-1, keepdims=True)
    e2 = jnp.exp(m2 - m1)
    w1 = p_grp / (1.0 + e2)
    w2 = p_grp * e2 / (1.0 + e2)
    e1 = i1 - N_GROUPS
    e2i = i2 - N_GROUPS

    oh1 = lane == e1
    oh2 = lane == e2i
    oh = jnp.where(oh1, 1.0, jnp.where(oh2, 1.0, 0.0))
    r = lax.broadcasted_iota(I32, (tm, tm), 0)
    c = lax.broadcasted_iota(I32, (tm, tm), 1)
    before = _dot(jnp.where(r > c, 1.0, 0.0).astype(BF16), oh.astype(BF16)) + carry[...]
    rank1 = jnp.sum(jnp.where(oh1, before, 0.0), axis=-1, keepdims=True)
    rank2 = jnp.sum(jnp.where(oh2, before, 0.0), axis=-1, keepdims=True)
    carry[...] = carry[...] + jnp.sum(oh, axis=0, keepdims=True)
    cnt_ref[...] = jnp.broadcast_to(carry[...], cnt_ref.shape)
    ri = jnp.where(lane == 0, e1, jnp.where(lane == 1, e2i, jnp.where(lane == 2, rank1,
                                                                      jnp.where(lane == 3, rank2, 0.0))))
    ri_ref[...] = ri.astype(I32)
    rw_ref[...] = jnp.where(lane == 0, w1, jnp.where(lane == 1, w2, 0.0))


def _merge(x, attn, c, o, pooled, lw):
    T = x.shape[0]
    tm = min(512, T)
    tile = lambda n: pl.BlockSpec((tm, n), lambda i: (i, 0))
    full = lambda a: pl.BlockSpec(a.shape, lambda i: (0,) * a.ndim)
    weights = [lw["norm_mix_g"], lw["w_gate"], lw["w_branch_attn"], lw["w_branch_conf"], lw["w_branch_gdn"],
               lw["w_branch_pool"], lw["w_out"], lw["norm_ffn_g"], lw["w_route_hi"], lw["w_route_mid"],
               lw["w_route_lo"], lw["b_route"]]
    return pl.pallas_call(
        functools.partial(_merge_body, tm),
        grid=(T // tm,),
        in_specs=[tile(D_MODEL), tile(SWA_Q), tile(CONF_DIM), tile(GDN_HEADS * GDN_DV), tile(POOL_DIM)]
                 + [full(w) for w in weights],
        out_specs=[tile(D_MODEL), pl.BlockSpec((tm * ROW_SUB, LANES), lambda i: (i, 0)), tile(LANES), tile(LANES),
                   pl.BlockSpec((8, LANES), lambda i: (0, 0))],
        out_shape=[jax.ShapeDtypeStruct((T, D_MODEL), F32), jax.ShapeDtypeStruct((T * ROW_SUB, LANES), F32),
                   jax.ShapeDtypeStruct((T, LANES), I32), jax.ShapeDtypeStruct((T, LANES), F32),
                   jax.ShapeDtypeStruct((8, LANES), F32)],
        scratch_shapes=[pltpu.VMEM((1, LANES), F32)],
        compiler_params=_params("arbitrary"),
        name="merge_route",
    )(x, attn, c, o, pooled, *weights)


def _row_tile(ref, r):
    return ref.at[pl.ds(pl.multiple_of(r * ROW_SUB, ROW_SUB), ROW_SUB), :]


def _row_copy(src, s_row, dst, d_row, sem):
    return pltpu.make_async_copy(_row_tile(src, s_row), _row_tile(dst, d_row), sem)


def _load_row_tiles(ref, n, first=0):
    return jnp.concatenate([ref[pl.ds(first * ROW_SUB + s, n, stride=ROW_SUB), :] for s in range(ROW_SUB)], axis=1)


def _store_row_tiles(ref, x):
    for s in range(ROW_SUB):
        ref[pl.ds(s, x.shape[0], stride=ROW_SUB), :] = x[:, s * LANES:(s + 1) * LANES]


def _dispatch_body(tc, n_blocks, dest_ref, pend_ref, x_ref, xb_ref, zbuf, sem_z, sem_s):
    i = pl.program_id(0)

    def zero_block(start):
        return pltpu.make_async_copy(zbuf, xb_ref.at[pl.ds(pl.multiple_of(start * ROW_SUB, MOE_BLOCK), MOE_BLOCK * ROW_SUB), :], sem_z)

    def zero_copy(e):
        return zero_block(jnp.maximum(pend_ref[e] - MOE_BLOCK, 0))

    @pl.when(i == 0)
    def _():
        zbuf[...] = jnp.zeros_like(zbuf)
        n_used = pend_ref[N_EXPERTS - 1] // MOE_BLOCK

        @pl.loop(0, N_EXPERTS)
        def _(e):
            zero_copy(e).start()

        @pl.loop(n_used, n_blocks)
        def _(b):
            zero_block(b * MOE_BLOCK).start()

        @pl.loop(0, N_EXPERTS)
        def _(e):
            zero_copy(e).wait()

        @pl.loop(n_used, n_blocks)
        def _(b):
            zero_block(b * MOE_BLOCK).wait()

    def copies(r):
        slot = (i * tc + r) * 2
        return (_row_copy(x_ref, r, xb_ref, dest_ref[slot], sem_s),
                _row_copy(x_ref, r, xb_ref, dest_ref[slot + 1], sem_s))

    @pl.loop(0, tc)
    def _(r):
        for cp in copies(r):
            cp.start()

    @pl.loop(0, tc)
    def _(r):
        for cp in copies(r):
            cp.wait()


def _dispatch(xn2, dest_flat, pad_end, n_rows):
    T = xn2.shape[0] // ROW_SUB
    tc = min(512, T)
    return pl.pallas_call(
        functools.partial(_dispatch_body, tc, n_rows // MOE_BLOCK),
        grid_spec=pltpu.PrefetchScalarGridSpec(
            num_scalar_prefetch=2,
            grid=(T // tc,),
            in_specs=[pl.BlockSpec((tc * ROW_SUB, LANES), lambda i, d, p: (i, 0))],
            out_specs=pl.BlockSpec(memory_space=pl.ANY),
            scratch_shapes=[pltpu.VMEM((MOE_BLOCK * ROW_SUB, LANES), F32), pltpu.SemaphoreType.DMA(()),
                            pltpu.SemaphoreType.DMA(())]),
        out_shape=jax.ShapeDtypeStruct((n_rows * ROW_SUB, LANES), F32),
        compiler_params=_params("arbitrary"),
        name="moe_dispatch",
    )(dest_flat, pad_end, xn2)


def _expert_body(be_ref, nu_ref, x_ref, wg_ref, wu_ref, wd_ref, y_ref, wg_b, wu_b, wd_b):
    i = pl.program_id(0)

    @pl.when((i == 0) | (be_ref[i] != be_ref[jnp.maximum(i - 1, 0)]))
    def _():
        wg_b[...] = wg_ref[...].astype(BF16)
        wu_b[...] = wu_ref[...].astype(BF16)
        wd_b[...] = wd_ref[...].astype(BF16)

    @pl.when(i < nu_ref[0])
    def _():
        x = _load_row_tiles(x_ref, MOE_BLOCK).astype(BF16)
        a = _dot(x, wg_b[...])
        h = (a * _sigmoid(a)) * _dot(x, wu_b[...])
        _store_row_tiles(y_ref, _dot(h.astype(BF16), wd_b[...]))

    @pl.when(i >= nu_ref[0])
    def _():
        y_ref[...] = jnp.zeros_like(y_ref)


def _experts(xb, block_expert, n_used, lw):
    n_rows = xb.shape[0] // ROW_SUB
    n_blocks = n_rows // MOE_BLOCK
    layer = lw["layer"]
    blk = (MOE_BLOCK * ROW_SUB, LANES)
    return pl.pallas_call(
        _expert_body,
        grid_spec=pltpu.PrefetchScalarGridSpec(
            num_scalar_prefetch=2,
            grid=(n_blocks,),
            in_specs=[pl.BlockSpec(blk, lambda i, be, nu: (jnp.minimum(i, nu[0] - 1), 0)),
                      pl.BlockSpec((None, None, D_MODEL, D_EXPERT), lambda i, be, nu: (layer, be[i], 0, 0)),
                      pl.BlockSpec((None, None, D_MODEL, D_EXPERT), lambda i, be, nu: (layer, be[i], 0, 0)),
                      pl.BlockSpec((None, None, D_EXPERT, D_MODEL), lambda i, be, nu: (layer, be[i], 0, 0))],
            out_specs=pl.BlockSpec(blk, lambda i, be, nu: (i, 0)),
            scratch_shapes=[pltpu.VMEM((D_MODEL, D_EXPERT), BF16), pltpu.VMEM((D_MODEL, D_EXPERT), BF16),
                            pltpu.VMEM((D_EXPERT, D_MODEL), BF16)]),
        out_shape=jax.ShapeDtypeStruct((n_rows * ROW_SUB, LANES), F32),
        compiler_params=_params("arbitrary"),
        name="moe_experts",
    )(block_expert, n_used, xb, lw["w_exp_gate"], lw["w_exp_up"], lw["w_exp_down"])


def _combine_body(tc, final, dest_ref, x_ref, w_ref, g_ref, yb_ref, out_ref, ybuf, sem):
    i = pl.program_id(0)

    def copies(step, buf, r):
        tok = step * tc + r
        return (_row_copy(yb_ref, dest_ref[2 * tok], ybuf.at[buf], r, sem.at[buf]),
                _row_copy(yb_ref, dest_ref[2 * tok + 1], ybuf.at[buf], tc + r, sem.at[buf]))

    def start_step(step, buf):
        @pl.loop(0, tc)
        def _(r):
            for cp in copies(step, buf, r):
                cp.start()

    @pl.when(i == 0)
    def _():
        start_step(i, 0)

    for buf in range(2):
        @pl.when(i % 2 == buf)
        def _():
            @pl.when(i + 1 < pl.num_programs(0))
            def _():
                start_step(i + 1, 1 - buf)

            @pl.loop(0, tc)
            def _(r):
                for cp in copies(i, buf, r):
                    cp.wait()

            w = w_ref[...]
            y0 = _load_row_tiles(ybuf.at[buf], tc)
            y1 = _load_row_tiles(ybuf.at[buf], tc, first=tc)
            out = x_ref[...] + (y0 * w[:, 0:1] + y1 * w[:, 1:2])
            if final:
                out = _rms(out, g_ref[...])
            out_ref[...] = out


def _combine(x_new, route_w, yb, dest_flat, g_final, final):
    T = x_new.shape[0]
    tc = min(256, T)
    return pl.pallas_call(
        functools.partial(_combine_body, tc, final),
        grid_spec=pltpu.PrefetchScalarGridSpec(
            num_scalar_prefetch=1,
            grid=(T // tc,),
            in_specs=[pl.BlockSpec((tc, D_MODEL), lambda i, d: (i, 0)),
                      pl.BlockSpec((tc, LANES), lambda i, d: (i, 0)),
                      pl.BlockSpec((1, D_MODEL), lambda i, d: (0, 0)),
                      pl.BlockSpec(memory_space=pl.ANY)],
            out_specs=pl.BlockSpec((tc, D_MODEL), lambda i, d: (i, 0)),
            scratch_shapes=[pltpu.VMEM((2, 2 * tc * ROW_SUB, LANES), F32), pltpu.SemaphoreType.DMA((2,))]),
        out_shape=jax.ShapeDtypeStruct((T, D_MODEL), F32),
        compiler_params=_params("arbitrary"),
        name="moe_combine",
    )(dest_flat, x_new, route_w, g_final, yb)


def _moe(x_new, xn2, route_i, route_w, counts_f, lw, g_final, final):
    T = x_new.shape[0]
    counts = counts_f[0, :N_EXPERTS].astype(I32)
    padded = (counts + MOE_BLOCK - 1) // MOE_BLOCK * MOE_BLOCK
    pad_end = jnp.cumsum(padded)
    pad_start = pad_end - padded
    eid = route_i[:, 0:2]
    onehot = eid[:, :, None] == jnp.arange(N_EXPERTS, dtype=I32)[None, None, :]
    dest = route_i[:, 2:4] + jnp.sum(jnp.where(onehot, pad_start[None, None, :], 0), axis=-1)
    dest_flat = dest.reshape(-1).astype(I32)
    n_blocks = (2 * T + N_EXPERTS * (MOE_BLOCK - 1)) // MOE_BLOCK
    n_used = (pad_end[N_EXPERTS - 1:] // MOE_BLOCK).astype(I32)
    block_start = jnp.arange(n_blocks, dtype=I32) * MOE_BLOCK
    block_expert = jnp.minimum(jnp.sum(block_start[:, None] >= pad_end[None, :], axis=-1), N_EXPERTS - 1).astype(I32)
    xb = _dispatch(xn2, dest_flat, pad_end.astype(I32), n_blocks * MOE_BLOCK)
    yb = _experts(xb, block_expert, n_used, lw)
    return _combine(x_new, route_w, yb, dest_flat, g_final, final)


def _layer_weights(p, l):
    w_in = p["w_in"][l]
    q_w = w_in[:, :SWA_Q].reshape(D_MODEL, SWA_KV_HEADS, SWA_GROUP, HEAD_DIM).transpose(0, 2, 1, 3)
    off = SWA_Q + 2 * SWA_KV
    conf_w = w_in[:, off:off + 2 * CONF_DIM]
    off += 2 * CONF_DIM
    gqkv_w = w_in[:, off:off + 3 * GDN_DIM]
    off += 3 * GDN_DIM
    gz_w = w_in[:, off:off + GDN_HEADS * GDN_DV]
    off += GDN_HEADS * GDN_DV
    gba_w = jnp.pad(w_in[:, off:off + 2 * GDN_HEADS], ((0, 0), (0, LANES - 2 * GDN_HEADS)))
    off += 2 * GDN_HEADS
    pool_w_in = w_in[:, off:off + POOL_DIM]
    off += POOL_DIM
    w_small = jnp.concatenate([q_w.reshape(D_MODEL, SWA_Q), w_in[:, SWA_Q:SWA_Q + 2 * SWA_KV], conf_w, gqkv_w, gz_w,
                               pool_w_in, gba_w], axis=1).astype(BF16)
    pool_bd = jnp.zeros((POOL_DIM, POOL_DIM), F32)
    for gi in range(len(POOL_WINDOWS)):
        sl = slice(gi * POOL_GROUP_DIM, (gi + 1) * POOL_GROUP_DIM)
        pool_bd = pool_bd.at[sl, sl].set(p["pool_w"][l, gi])
    lane_pad = lambda v: jnp.pad(v, (GDN_HEADS, LANES - 2 * GDN_HEADS))[None, :]
    w_route = jnp.pad(jnp.concatenate([p["w_group"][l], p["w_router"][l]], axis=1),
                      ((0, 0), (0, LANES - N_GROUPS - N_EXPERTS)))
    r_hi = w_route.astype(BF16)
    r_mid = (w_route - r_hi.astype(F32)).astype(BF16)
    r_lo = (w_route - r_hi.astype(F32) - r_mid.astype(F32)).astype(BF16)
    return {
        "norm_mix_g": p["norm_mix_g"][l][None, :],
        "w_small": w_small,
        "w_gate": w_in[:, off:].astype(BF16),
        "swa_sinks": p["swa_sinks"][l],
        "conf_w": jnp.pad(p["conf_dw_w"][l], ((0, CONF_HIST - CONF_KERNEL), (0, 0))),
        "conf_b": p["conf_dw_b"][l][None, :],
        "conf_ln_g": p["conf_ln_g"][l][None, :],
        "conf_ln_b": p["conf_ln_b"][l][None, :],
        "gdn_conv_w": jnp.pad(p["gdn_conv_w"][l], ((0, GDN_HIST - GDN_CONV), (0, 0))),
        "gdn_a_log": lane_pad(p["gdn_a_log"][l]),
        "gdn_dt_bias": lane_pad(p["gdn_dt_bias"][l]),
        "gdn_norm_g": jnp.tile(p["gdn_norm_g"][l], 2)[None, :],
        "pool_w": pool_bd.astype(BF16),
        "pool_scale": p["pool_scale"][l][None, :],
        "w_branch_attn": p["w_branch_attn"][l].reshape(SWA_KV_HEADS, SWA_GROUP, HEAD_DIM, D_MODEL)
                         .transpose(1, 0, 2, 3).reshape(SWA_Q, D_MODEL).astype(BF16),
        "w_branch_conf": p["w_branch_conf"][l].astype(BF16),
        "w_branch_gdn": p["w_branch_gdn"][l].astype(BF16),
        "w_branch_pool": p["w_branch_pool"][l].astype(BF16),
        "w_out": p["w_out"][l].astype(BF16),
        "norm_ffn_g": p["norm_ffn_g"][l][None, :],
        "w_route_hi": r_hi, "w_route_mid": r_mid, "w_route_lo": r_lo,
        "b_route": jnp.pad(jnp.concatenate([p["b_group"][l], p["b_router"][l]]),
                           (0, LANES - N_GROUPS - N_EXPERTS))[None, :],
        "layer": l,
        "w_exp_gate": p["w_exp_gate"],
        "w_exp_up": p["w_exp_up"],
        "w_exp_down": p["w_exp_down"],
    }


def _pad_rows(a, n, front=False):
    extra = n - a.shape[1]
    return jnp.pad(a, ((0, 0), (extra, 0) if front else (0, extra), (0, 0)))


def _pair_state(s):
    z = jnp.zeros_like(s[:, 0::2])
    top = jnp.concatenate([s[:, 0::2], z], axis=-1)
    bot = jnp.concatenate([z, s[:, 1::2]], axis=-1)
    return jnp.concatenate([top, bot], axis=-2)


def _unpair_state(sp):
    even = sp[:, :, :GDN_DK, :GDN_DV]
    odd = sp[:, :, GDN_DK:, GDN_DV:]
    return jnp.stack([even, odd], axis=2).reshape(sp.shape[0], GDN_HEADS, GDN_DK, GDN_DV)


def _token_mixer(x, lw, state, start_pos):
    B, L, _ = x.shape
    T = B * L
    q, kv, conf_in, gqkv, gz, pool_in, gba = _proj(x.reshape(T, D_MODEL), lw["norm_mix_g"], lw["w_small"])
    seq = lambda a: a.reshape(B, L, a.shape[-1])
    q, kv, conf_in, gqkv, gz, pool_in, gba = map(seq, (q, kv, conf_in, gqkv, gz, pool_in, gba))
    has_past = state is not None
    if has_past:
        past_k, past_v, conf_buf, gdn_buf, gdn_s, pool_buf = state
        kv0 = jnp.concatenate([past_k.reshape(B, WINDOW, SWA_KV), past_v.reshape(B, WINDOW, SWA_KV)], axis=-1)
        ll, lg = STEP_PAD, CHUNK
    else:
        conf_buf = jnp.zeros((B, CONF_KERNEL - 1, CONF_DIM), F32)
        gdn_buf = jnp.zeros((B, GDN_CONV - 1, 3 * GDN_DIM), F32)
        gdn_s = jnp.zeros((B, GDN_HEADS, GDN_DK, GDN_DV), F32)
        pool_buf = jnp.zeros((B, POOL_MAX - 1, POOL_DIM), F32)
        ll = lg = L
    if has_past:
        attn = _attn_step(_pad_rows(q, STEP_PAD), _pad_rows(kv, STEP_PAD), kv0, lw["swa_sinks"])[:, :L]
    else:
        attn = _attn(q, kv, lw["swa_sinks"])
    c, pooled, cu_last = _local_mixers(_pad_rows(conf_in, ll), _pad_rows(pool_in, ll),
                                       _pad_rows(conf_buf, CONF_HIST, front=True),
                                       _pad_rows(pool_buf, POOL_HIST, front=True), lw, start_pos)
    o, s_fin = _gdn(_pad_rows(gqkv, lg), _pad_rows(gz, lg), _pad_rows(gba, lg),
                    _pad_rows(gdn_buf, GDN_HIST, front=True), _pair_state(gdn_s), lw, L)
    if has_past:
        new_kv = jnp.concatenate([kv0[:, L:], kv], axis=1)
        n_hist = cu_last.shape[1]
        new_conf = jnp.concatenate([conf_buf[:, L:], cu_last[:, n_hist - ll:n_hist - ll + L]], axis=1)
        new_gdn_buf = jnp.concatenate([gdn_buf, gqkv], axis=1)[:, -(GDN_CONV - 1):]
        new_pool = jnp.concatenate([pool_buf[:, L:], pool_in], axis=1)
    else:
        new_kv = kv[:, -WINDOW:]
        new_conf = cu_last[:, -(CONF_KERNEL - 1):]
        new_gdn_buf = gqkv[:, -(GDN_CONV - 1):]
        new_pool = pool_in[:, -(POOL_MAX - 1):]
    new_k = new_kv[..., :SWA_KV].reshape(B, WINDOW, SWA_KV_HEADS, HEAD_DIM)
    new_v = new_kv[..., SWA_KV:].reshape(B, WINDOW, SWA_KV_HEADS, HEAD_DIM)
    flat = lambda a: a[:, :L].reshape(T, a.shape[-1])
    branches = (flat(attn), flat(c), flat(o), flat(pooled))
    return branches, (new_k, new_v, new_conf, new_gdn_buf, _unpair_state(s_fin), new_pool)


def _trunk(x, past, start_pos, layers, g_final):
    B, L, _ = x.shape
    collected = [[] for _ in range(6)]
    xf = x.reshape(B * L, D_MODEL)
    for l, lw in enumerate(layers):
        state = None if past is None else tuple(p[l] for p in past)
        branches, new = _token_mixer(xf.reshape(B, L, D_MODEL), lw, state, start_pos)
        x_new, xn2, route_i, route_w, counts = _merge(xf, *branches, lw)
        xf = _moe(x_new, xn2, route_i, route_w, counts, lw, g_final, l == len(layers) - 1)
        for lst, arr in zip(collected, new):
            lst.append(arr)
    return xf.reshape(B, L, D_MODEL), [jnp.stack(lst, axis=0) for lst in collected]


def kernel(x_prompt, x_sample, cache_swa_k, cache_swa_v, state_conf_conv, state_gdn_conv, state_gdn, state_pool, norm_mix_g, w_in, swa_sinks, conf_dw_w, conf_dw_b, conf_ln_g, conf_ln_b, gdn_conv_w, gdn_a_log, gdn_dt_bias, gdn_norm_g, pool_w, pool_scale, w_branch_attn, w_branch_conf, w_branch_gdn, w_branch_pool, w_out, norm_ffn_g, w_group, b_group, w_router, b_router, w_exp_gate, w_exp_up, w_exp_down, final_norm_g):
    params = dict(norm_mix_g=norm_mix_g, w_in=w_in, swa_sinks=swa_sinks, conf_dw_w=conf_dw_w,
                  conf_dw_b=conf_dw_b, conf_ln_g=conf_ln_g, conf_ln_b=conf_ln_b, gdn_conv_w=gdn_conv_w,
                  gdn_a_log=gdn_a_log, gdn_dt_bias=gdn_dt_bias, gdn_norm_g=gdn_norm_g, pool_w=pool_w,
                  pool_scale=pool_scale, w_branch_attn=w_branch_attn, w_branch_conf=w_branch_conf,
                  w_branch_gdn=w_branch_gdn, w_branch_pool=w_branch_pool, w_out=w_out, norm_ffn_g=norm_ffn_g,
                  w_group=w_group, b_group=b_group, w_router=w_router, b_router=b_router,
                  w_exp_gate=w_exp_gate, w_exp_up=w_exp_up, w_exp_down=w_exp_down)
    layers = [_layer_weights(params, l) for l in range(w_in.shape[0])]
    g_final = final_norm_g[None, :]
    yp, (pk, pv, pc, pgc, pgs, pp) = _trunk(x_prompt, None, 0, layers, g_final)
    past = (cache_swa_k, cache_swa_v, state_conf_conv, state_gdn_conv, state_gdn, state_pool)
    ys, (sk, sv, sc, sgc, sgs, sp) = _trunk(x_sample, past, PAST_LEN, layers, g_final)
    return (yp, ys, pk, pv, pc, pgc, pgs, pp, sk, sv, sc, sgc, sgs, sp)
```

```python
import functools

import jax
import jax.numpy as jnp
from jax import lax
from jax.experimental import pallas as pl
from jax.experimental.pallas import tpu as pltpu

F32 = jnp.float32
BF16 = jnp.bfloat16
I32 = jnp.int32

D_MODEL = 1024
DEPTH = 2
PAST_LEN = 16384
HEAD_DIM = 64
SWA_HEADS = 6
SWA_KV_HEADS = 2
SWA_GROUP = SWA_HEADS // SWA_KV_HEADS
SWA_Q = SWA_HEADS * HEAD_DIM
SWA_KV = SWA_KV_HEADS * HEAD_DIM
WINDOW = 128
ATTN_SCALE = HEAD_DIM ** -0.5
CONF_DIM = D_MODEL // 4
CONF_KERNEL = 31
GDN_HEADS = 4
GDN_DK = 64
GDN_DV = 64
GDN_DIM = GDN_HEADS * GDN_DK
GDN_CONV = 4
POOL_WINDOWS = (2, 4, 8, 16)
POOL_DIM = D_MODEL // 4
POOL_GROUP_DIM = POOL_DIM // len(POOL_WINDOWS)
POOL_MAX = max(POOL_WINDOWS)
N_BRANCHES = 4
N_GROUPS = 4
EXPERTS_PER_GROUP = 8
N_EXPERTS = N_GROUPS * EXPERTS_PER_GROUP
D_EXPERT = D_MODEL // 2
MOE_BLOCK = 256
EPS = 1e-6

LANES = 128
ROW_SUB = D_MODEL // LANES
CHUNK = 128
CONF_HIST = 32
POOL_HIST = 16
GDN_HIST = 8
GDN_ROWS = 4
ATTN_STEP_ROWS = 8
STEP_PAD = 8
VMEM_LIMIT = 56 * 1024 * 1024

PROJ_SPLITS = (SWA_Q, 2 * SWA_KV, 2 * CONF_DIM, 3 * GDN_DIM, GDN_HEADS * GDN_DV, POOL_DIM, LANES)
W_IN_OFFS = (0, SWA_Q, SWA_Q + SWA_KV, SWA_Q + 2 * SWA_KV)


def _params(*sem):
    return pltpu.CompilerParams(dimension_semantics=sem, vmem_limit_bytes=VMEM_LIMIT)


def _rms(x, g):
    return x * lax.rsqrt(jnp.mean(x * x, axis=-1, keepdims=True) + EPS) * g


def _sigmoid(x):
    return 1.0 / (1.0 + jnp.exp(-x))


def _dot(a, b):
    return jnp.dot(a, b, preferred_element_type=F32)


def _dot_nt(a, b):
    return lax.dot_general(a, b, (((1,), (1,)), ((), ())), preferred_element_type=F32)


def _split3(x):
    hi = x.astype(BF16)
    r = x - hi.astype(F32)
    mid = r.astype(BF16)
    lo = (r - mid.astype(F32)).astype(BF16)
    return hi, mid, lo


def _dot_exact_lhs(a_bf16, b):
    hi, mid, lo = _split3(b)
    return _dot(a_bf16, hi) + _dot(a_bf16, mid) + _dot(a_bf16, lo)


def _proj_body(x_ref, g_ref, w_ref, *out_refs):
    xn = _rms(x_ref[...], g_ref[...]).astype(BF16)
    off = 0
    for o_ref, n in zip(out_refs, PROJ_SPLITS):
        o_ref[...] = _dot(xn, w_ref[:, off:off + n])
        off += n


def _proj(x, g, w_small):
    T = x.shape[0]
    tm = min(512, T)
    n_small = sum(PROJ_SPLITS)
    return pl.pallas_call(
        _proj_body,
        grid=(T // tm,),
        in_specs=[pl.BlockSpec((tm, D_MODEL), lambda i: (i, 0)),
                  pl.BlockSpec((1, D_MODEL), lambda i: (0, 0)),
                  pl.BlockSpec((D_MODEL, n_small), lambda i: (0, 0))],
        out_specs=[pl.BlockSpec((tm, n), lambda i: (i, 0)) for n in PROJ_SPLITS],
        out_shape=[jax.ShapeDtypeStruct((T, n), F32) for n in PROJ_SPLITS],
        compiler_params=_params("parallel"),
        name="proj",
    )(x, g, w_small)


def _attn_body(sink_ref, q_ref, kv_ref, kvp_ref, o_ref):
    i = pl.program_id(1)
    q = q_ref[...] * ATTN_SCALE
    kvc = kv_ref[...]
    kvp = kvp_ref[...]
    k2 = jnp.concatenate([kvp[:, :LANES], kvc[:, :LANES]], axis=0).astype(BF16)
    v2 = jnp.concatenate([kvp[:, LANES:], kvc[:, LANES:]], axis=0).astype(BF16)
    lane = lax.broadcasted_iota(I32, (WINDOW, LANES), 1)
    row = lax.broadcasted_iota(I32, (WINDOW, 2 * WINDOW), 0)
    col = lax.broadcasted_iota(I32, (WINDOW, 2 * WINDOW), 1)
    valid = (col > row) & (col <= row + WINDOW) & ((col >= WINDOW) | (i > 0))
    for g in range(SWA_GROUP):
        qg = q[:, g * LANES:(g + 1) * LANES]
        og = jnp.zeros((WINDOW, LANES), F32)
        for j in range(SWA_KV_HEADS):
            half = (lane >= HEAD_DIM) if j == 1 else (lane < HEAD_DIM)
            qj = jnp.where(half, qg, 0.0).astype(BF16)
            s = jnp.where(valid, _dot_nt(qj, k2), -jnp.inf)
            sink = sink_ref[j * SWA_GROUP + g]
            m = jnp.maximum(jnp.max(s, axis=-1, keepdims=True), sink)
            p = jnp.exp(s - m)
            denom = jnp.sum(p, axis=-1, keepdims=True) + jnp.exp(sink - m)
            pv = _dot(p.astype(BF16), v2)
            og = jnp.where(half, pv / denom, og)
        o_ref[:, g * LANES:(g + 1) * LANES] = og.astype(BF16)


def _attn(q, kv, sinks):
    B, L, _ = q.shape
    nb = L // WINDOW
    return pl.pallas_call(
        _attn_body,
        grid=(B, nb),
        in_specs=[pl.BlockSpec(memory_space=pltpu.SMEM),
                  pl.BlockSpec((None, WINDOW, SWA_Q), lambda b, i: (b, i, 0)),
                  pl.BlockSpec((None, WINDOW, 2 * SWA_KV), lambda b, i: (b, i, 0)),
                  pl.BlockSpec((None, WINDOW, 2 * SWA_KV), lambda b, i: (b, jnp.maximum(i - 1, 0), 0))],
        out_specs=pl.BlockSpec((None, WINDOW, SWA_Q), lambda b, i: (b, i, 0)),
        out_shape=jax.ShapeDtypeStruct((B, L, SWA_Q), BF16),
        compiler_params=_params("parallel", "arbitrary"),
        name="swa",
    )(sinks, q, kv, kv)


def _attn_step_body(sink_ref, q_ref, kv_ref, kv0_ref, o_ref):
    n_seq, tq, _ = q_ref.shape
    heads = [(g, j) for g in range(SWA_GROUP) for j in range(SWA_KV_HEADS)]
    nq = len(heads) * tq
    row = lax.broadcasted_iota(I32, (nq, LANES), 0)
    col = lax.broadcasted_iota(I32, (nq, LANES), 1)
    tok = lax.rem(row, tq)
    in_cache = col > tok
    causal = col <= tok
    left = lax.broadcasted_iota(I32, (tq, LANES), 1) < HEAD_DIM
    sink = jnp.zeros((nq, LANES), F32)
    for h, (g, j) in enumerate(heads):
        sink = jnp.where(row // tq == h, sink_ref[j * SWA_GROUP + g], sink)
    zpad = jnp.zeros((WINDOW - tq, LANES), F32)
    seqs = range(n_seq)
    q = [q_ref[b] * ATTN_SCALE for b in seqs]
    qs = [jnp.concatenate([jnp.where(left if j == 0 else jnp.logical_not(left), q[b][:, g * LANES:(g + 1) * LANES], 0.0)
                           for (g, j) in heads], axis=0).astype(BF16) for b in seqs]
    kp = [kv0_ref[b, :, :LANES].astype(BF16) for b in seqs]
    vp = [kv0_ref[b, :, LANES:].astype(BF16) for b in seqs]
    kc = [jnp.concatenate([kv_ref[b, :, :LANES], zpad], axis=0).astype(BF16) for b in seqs]
    vc = [jnp.concatenate([kv_ref[b, :, LANES:], zpad], axis=0).astype(BF16) for b in seqs]
    sp = [jnp.where(in_cache, _dot_nt(qs[b], kp[b]), -jnp.inf) for b in seqs]
    sc = [jnp.where(causal, _dot_nt(qs[b], kc[b]), -jnp.inf) for b in seqs]
    m = [jnp.maximum(jnp.maximum(jnp.max(sp[b], axis=-1, keepdims=True), jnp.max(sc[b], axis=-1, keepdims=True)), sink)
         for b in seqs]
    pp = [jnp.exp(sp[b] - m[b]) for b in seqs]
    pc = [jnp.exp(sc[b] - m[b]) for b in seqs]
    denom = [jnp.sum(pp[b], axis=-1, keepdims=True) + jnp.sum(pc[b], axis=-1, keepdims=True) + jnp.exp(sink - m[b])
             for b in seqs]
    pv = [(_dot(pp[b].astype(BF16), vp[b]) + _dot(pc[b].astype(BF16), vc[b])) / denom[b] for b in seqs]
    for b in seqs:
        for g in range(SWA_GROUP):
            r0 = g * SWA_KV_HEADS * tq
            og = jnp.where(left, pv[b][r0:r0 + tq], pv[b][r0 + tq:r0 + 2 * tq])
            o_ref[b, :, g * LANES:(g + 1) * LANES] = og.astype(BF16)


def _attn_step(q, kv, kv0, sinks):
    B, tq, _ = q.shape
    G = ATTN_STEP_ROWS
    return pl.pallas_call(
        _attn_step_body,
        grid=(B // G,),
        in_specs=[pl.BlockSpec(memory_space=pltpu.SMEM),
                  pl.BlockSpec((G, tq, SWA_Q), lambda b: (b, 0, 0)),
                  pl.BlockSpec((G, tq, 2 * SWA_KV), lambda b: (b, 0, 0)),
                  pl.BlockSpec((G, WINDOW, 2 * SWA_KV), lambda b: (b, 0, 0))],
        out_specs=pl.BlockSpec((G, tq, SWA_Q), lambda b: (b, 0, 0)),
        out_shape=jax.ShapeDtypeStruct((B, tq, SWA_Q), BF16),
        compiler_params=_params("parallel"),
        name="swa_step",
    )(sinks, q, kv, kv0)


def _local_body(tl, start_pos, conf_ref, pool_ref, chist_ref, phist_ref, cw_ref, cb_ref, lng_ref, lnb_ref,
                pw_ref, ps_ref, c_ref, pooled_ref, culast_ref, cbuf, pbuf):
    for g in range(conf_ref.shape[0]):
        _local_row(tl, start_pos, conf_ref.at[g], pool_ref.at[g], chist_ref.at[g], phist_ref.at[g], cw_ref, cb_ref,
                   lng_ref, lnb_ref, pw_ref, ps_ref, c_ref.at[g], pooled_ref.at[g], culast_ref.at[g], cbuf.at[g],
                   pbuf.at[g])


def _local_row(tl, start_pos, conf_ref, pool_ref, chist_ref, phist_ref, cw_ref, cb_ref, lng_ref, lnb_ref,
               pw_ref, ps_ref, c_ref, pooled_ref, culast_ref, cbuf, pbuf):
    l = pl.program_id(1)

    @pl.when(l == 0)
    def _():
        cbuf[0:CONF_HIST, :] = chist_ref[...]
        pbuf[0:POOL_HIST, :] = phist_ref[...]

    ci = conf_ref[...]
    cbuf[CONF_HIST:CONF_HIST + tl, :] = ci[:, :CONF_DIM] * _sigmoid(ci[:, CONF_DIM:])
    base = CONF_HIST - (CONF_KERNEL - 1)
    acc = cw_ref[0:1, :] * cbuf[base:base + tl, :]
    for j in range(1, CONF_KERNEL):
        acc = acc + cw_ref[j:j + 1, :] * cbuf[base + j:base + j + tl, :]
    y = acc + cb_ref[...]
    yc = y - jnp.mean(y, axis=-1, keepdims=True)
    yn = yc * lax.rsqrt(jnp.mean(yc * yc, axis=-1, keepdims=True) + EPS) * lng_ref[...] + lnb_ref[...]
    c_ref[...] = (yn * _sigmoid(yn)).astype(BF16)
    tail = cbuf[tl:tl + CONF_HIST, :]
    culast_ref[...] = tail
    cbuf[0:CONF_HIST, :] = tail

    u = pool_ref[...]
    pbuf[POOL_HIST:POOL_HIST + tl, :] = u
    lane = lax.broadcasted_iota(I32, (tl, POOL_DIM), 1)
    n_avail = (start_pos + 1 + l * tl + lax.broadcasted_iota(I32, (tl, POOL_DIM), 0)).astype(F32)
    wsum = u
    mean = None
    shift = 1
    for gi, wlen in enumerate(POOL_WINDOWS):
        while shift < wlen:
            wsum = wsum + pbuf[POOL_HIST - shift:POOL_HIST - shift + tl, :]
            shift += 1
        m = wsum / jnp.minimum(n_avail, float(wlen))
        mean = m if mean is None else jnp.where(lane >= gi * POOL_GROUP_DIM, m, mean)
    mixed = (mean - u).astype(BF16)
    pooled_ref[...] = (_dot(mixed, pw_ref[...]) * ps_ref[...]).astype(BF16)
    pbuf[0:POOL_HIST, :] = pbuf[tl:tl + POOL_HIST, :]


def _local_mixers(conf_in, pool_in, conf_hist, pool_hist, lw, start_pos):
    B, L, _ = conf_in.shape
    tl = min(512, L)
    G = 1 if L > STEP_PAD else ATTN_STEP_ROWS
    row = lambda n: pl.BlockSpec((1, n), lambda b, l: (0, 0))
    return pl.pallas_call(
        functools.partial(_local_body, tl, start_pos),
        grid=(B // G, L // tl),
        in_specs=[pl.BlockSpec((G, tl, 2 * CONF_DIM), lambda b, l: (b, l, 0)),
                  pl.BlockSpec((G, tl, POOL_DIM), lambda b, l: (b, l, 0)),
                  pl.BlockSpec((G, CONF_HIST, CONF_DIM), lambda b, l: (b, 0, 0)),
                  pl.BlockSpec((G, POOL_HIST, POOL_DIM), lambda b, l: (b, 0, 0)),
                  pl.BlockSpec((CONF_HIST, CONF_DIM), lambda b, l: (0, 0)),
                  row(CONF_DIM), row(CONF_DIM), row(CONF_DIM),
                  pl.BlockSpec((POOL_DIM, POOL_DIM), lambda b, l: (0, 0)),
                  row(POOL_DIM)],
        out_specs=[pl.BlockSpec((G, tl, CONF_DIM), lambda b, l: (b, l, 0)),
                   pl.BlockSpec((G, tl, POOL_DIM), lambda b, l: (b, l, 0)),
                   pl.BlockSpec((G, CONF_HIST, CONF_DIM), lambda b, l: (b, 0, 0))],
        out_shape=[jax.ShapeDtypeStruct((B, L, CONF_DIM), BF16),
                   jax.ShapeDtypeStruct((B, L, POOL_DIM), BF16),
                   jax.ShapeDtypeStruct((B, CONF_HIST, CONF_DIM), F32)],
        scratch_shapes=[pltpu.VMEM((G, tl + CONF_HIST, CONF_DIM), F32),
                        pltpu.VMEM((G, tl + POOL_HIST, POOL_DIM), F32)],
        compiler_params=_params("parallel", "arbitrary"),
        name="conv_pool",
    )(conf_in, pool_in, conf_hist, pool_hist, lw["conf_w"], lw["conf_b"], lw["conf_ln_g"], lw["conf_ln_b"],
      lw["pool_w"], lw["pool_scale"])


def _unit_lower_inverses(mats, order):
    row = lax.broadcasted_iota(I32, (CHUNK, CHUNK), 0)
    col = lax.broadcasted_iota(I32, (CHUNK, CHUNK), 1)
    eye = jnp.where(row == col, 1.0, 0.0)
    ps = [-a for a in mats]
    ts = [eye + n for n in ps]
    for _ in range(max((order - 1).bit_length() - 1, 0)):
        pbs = [p.astype(BF16) for p in ps]
        ps = [_dot(pb, pb) for pb in pbs]
        ts = [t + _dot(t.astype(BF16), p.astype(BF16)) for t, p in zip(ts, ps)]
    splits = [(_split3(t), _split3(eye + a)) for t, a in zip(ts, mats)]
    resids = [eye - (_dot(mh, th) + (_dot(mh, tm) + _dot(mm, th))) for (th, tm, _), (mh, mm, _) in splits]
    return [t + _dot(sp[0][0], r.astype(BF16)) for t, sp, r in zip(ts, splits, resids)]


def _gdn_body(n_valid, seq_len, gqkv_ref, gz_ref, gba_ref, hist_ref, s0_ref, cw_ref, alog_ref, dtb_ref, ng_ref,
              o_ref, sfin_ref, cbuf, s_scr):
    l = pl.program_id(1)
    rows = range(GDN_ROWS)
    pairs = [(g, p) for g in rows for p in range(GDN_HEADS // 2)]
    heads = [(g, p, hh) for (g, p) in pairs for hh in range(2)]

    @pl.when(l == 0)
    def _():
        cbuf[:, 0:GDN_HIST, :] = hist_ref[...]
        s_scr[...] = s0_ref[...]

    row = lax.broadcasted_iota(I32, (CHUNK, LANES), 0)
    col = lax.broadcasted_iota(I32, (CHUNK, LANES), 1)
    left = col < GDN_DK
    right = jnp.logical_not(left)
    lower = row >= col
    strict = row > col
    blockdiag = (row < GDN_DK) == left

    def halfsum(x):
        sl = jnp.sum(jnp.where(left, x, 0.0), axis=-1, keepdims=True)
        sr = jnp.sum(jnp.where(left, 0.0, x), axis=-1, keepdims=True)
        return jnp.where(left, sl, sr)

    def pair_cols(arr, lane0):
        return jnp.where(left, arr[:, lane0:lane0 + 1], arr[:, lane0 + 1:lane0 + 2])

    base = GDN_HIST - (GDN_CONV - 1)
    qkv = {}
    for g in rows:
        cbuf[g, GDN_HIST:GDN_HIST + CHUNK, :] = gqkv_ref[g]
        acc = cw_ref[0:1, :] * cbuf[g, base:base + CHUNK, :]
        for j in range(1, GDN_CONV):
            acc = acc + cw_ref[j:j + 1, :] * cbuf[g, base + j:base + j + CHUNK, :]
        cbuf[g, 0:GDN_HIST, :] = cbuf[g, CHUNK:CHUNK + GDN_HIST, :]
        qkv[g] = acc * _sigmoid(acc)

    ltri = jnp.where(lower, 1.0, 0.0).astype(BF16)
    beta_all, g_parts = {}, {}
    for g in rows:
        gba = gba_ref[g]
        beta = _sigmoid(gba)
        ga = gba + dtb_ref[...]
        softplus = jnp.maximum(ga, 0.0) + jnp.log(1.0 + jnp.exp(-jnp.abs(ga)))
        gl = -jnp.exp(alog_ref[...]) * softplus
        if n_valid < seq_len:
            live = (l * CHUNK + row) < n_valid
            beta = jnp.where(live, beta, 0.0)
            gl = jnp.where(live, gl, 0.0)
        beta_all[g] = beta
        g_parts[g] = _split3(gl)
    gc_all = {g: _dot(ltri, g_parts[g][0]) + _dot(ltri, g_parts[g][1]) + _dot(ltri, g_parts[g][2]) for g in rows}
    gc_t = {g: gc_all[g].T for g in rows}
    g_last = {g: gc_all[g][CHUNK - 1:CHUNK, :] for g in rows}
    e_gc = {g: jnp.exp(gc_all[g]) for g in rows}
    e_rest = {g: jnp.exp(g_last[g] - gc_all[g]) for g in rows}
    e_tot = {g: jnp.exp(g_last[g]) for g in rows}

    qn, kn, kn_b, kb, rhs, egc_p, erest_p = {}, {}, {}, {}, {}, {}, {}
    for (g, p) in pairs:
        h0 = 2 * p
        qp = qkv[g][:, p * LANES:(p + 1) * LANES]
        kp = qkv[g][:, GDN_DIM + p * LANES:GDN_DIM + (p + 1) * LANES]
        vp = qkv[g][:, 2 * GDN_DIM + p * LANES:2 * GDN_DIM + (p + 1) * LANES]
        qn[g, p] = qp * lax.rsqrt(halfsum(qp * qp) + EPS) * (GDN_DK ** -0.5)
        kn[g, p] = kp * lax.rsqrt(halfsum(kp * kp) + EPS)
        beta_p = pair_cols(beta_all[g], h0)
        egc_p[g, p] = pair_cols(e_gc[g], GDN_HEADS + h0)
        erest_p[g, p] = pair_cols(e_rest[g], GDN_HEADS + h0)
        kb[g, p] = kn[g, p] * beta_p
        rhs[g, p] = jnp.concatenate([vp * beta_p, kb[g, p] * egc_p[g, p]], axis=1).astype(BF16)
        kn_b[g, p] = kn[g, p].astype(BF16)

    decay, kk, qk = {}, {}, {}
    for (g, p, hh) in heads:
        lane = GDN_HEADS + 2 * p + hh
        hm = left if hh == 0 else right
        diff = gc_all[g][:, lane:lane + 1] - gc_t[g][lane:lane + 1, :]
        decay[g, p, hh] = jnp.where(lower, jnp.exp(jnp.where(lower, diff, 0.0)), 0.0)
        kk[g, p, hh] = _dot_nt(jnp.where(hm, kb[g, p], 0.0).astype(BF16), kn_b[g, p])
        qk[g, p, hh] = _dot_nt(jnp.where(hm, qn[g, p], 0.0).astype(BF16), kn_b[g, p])
    t_inv = _unit_lower_inverses([jnp.where(strict, kk[k] * decay[k], 0.0) for k in heads], min(CHUNK, n_valid))
    res = {k: _dot(t.astype(BF16), rhs[k[0], k[1]]) for k, t in zip(heads, t_inv)}
    qkb = {k: (qk[k] * decay[k]).astype(BF16) for k in heads}

    u = {k: jnp.where(left, res[k + (0,)][:, :LANES], res[k + (1,)][:, :LANES]) for k in pairs}
    w = {k: jnp.where(left, res[k + (0,)][:, LANES:], res[k + (1,)][:, LANES:]) for k in pairs}
    s = {k: s_scr[k[0], k[1]] for k in pairs}
    sb = {k: s[k].astype(BF16) for k in pairs}
    v_new = {k: u[k] - _dot(w[k].astype(BF16), sb[k]) for k in pairs}
    vnb = {k: v_new[k].astype(BF16) for k in pairs}
    o = {k: _dot((qn[k] * egc_p[k]).astype(BF16), sb[k])
         + jnp.where(left, _dot(qkb[k + (0,)], vnb[k]), _dot(qkb[k + (1,)], vnb[k])) for k in pairs}
    upd = {k: _dot((kn[k] * erest_p[k]).T.astype(BF16), vnb[k]) for k in pairs}
    for (g, p) in pairs:
        lane = GDN_HEADS + 2 * p
        gt = jnp.where(row < GDN_DK, e_tot[g][:, lane:lane + 1], e_tot[g][:, lane + 1:lane + 2])
        s_scr[g, p] = s[g, p] * gt + jnp.where(blockdiag, upd[g, p], 0.0)
        ok = o[g, p]
        on = ok * lax.rsqrt(halfsum(ok * ok) * (1.0 / GDN_DV) + EPS) * ng_ref[...]
        z = gz_ref[g, :, p * LANES:(p + 1) * LANES]
        o_ref[g, :, p * LANES:(p + 1) * LANES] = (on * (z * _sigmoid(z))).astype(BF16)
    sfin_ref[...] = s_scr[...]


def _gdn(gqkv, gz, gba, hist, s0, lw, n_valid):
    B, L, _ = gqkv.shape
    G = GDN_ROWS
    row = lambda n: pl.BlockSpec((1, n), lambda b, l: (0, 0))
    return pl.pallas_call(
        functools.partial(_gdn_body, n_valid, L),
        grid=(B // G, L // CHUNK),
        in_specs=[pl.BlockSpec((G, CHUNK, 3 * GDN_DIM), lambda b, l: (b, l, 0)),
                  pl.BlockSpec((G, CHUNK, GDN_HEADS * GDN_DV), lambda b, l: (b, l, 0)),
                  pl.BlockSpec((G, CHUNK, LANES), lambda b, l: (b, l, 0)),
                  pl.BlockSpec((G, GDN_HIST, 3 * GDN_DIM), lambda b, l: (b, 0, 0)),
                  pl.BlockSpec((G, 2, LANES, LANES), lambda b, l: (b, 0, 0, 0)),
                  pl.BlockSpec((GDN_HIST, 3 * GDN_DIM), lambda b, l: (0, 0)),
                  row(LANES), row(LANES), row(LANES)],
        out_specs=[pl.BlockSpec((G, CHUNK, GDN_HEADS * GDN_DV), lambda b, l: (b, l, 0)),
                   pl.BlockSpec((G, 2, LANES, LANES), lambda b, l: (b, 0, 0, 0))],
        out_shape=[jax.ShapeDtypeStruct((B, L, GDN_HEADS * GDN_DV), BF16),
                   jax.ShapeDtypeStruct((B, 2, LANES, LANES), F32)],
        scratch_shapes=[pltpu.VMEM((G, CHUNK + GDN_HIST, 3 * GDN_DIM), F32),
                        pltpu.VMEM((G, 2, LANES, LANES), F32)],
        compiler_params=_params("parallel", "arbitrary"),
        name="gdn",
    )(gqkv, gz, gba, hist, s0, lw["gdn_conv_w"], lw["gdn_a_log"], lw["gdn_dt_bias"], lw["gdn_norm_g"])


def _merge_body(tm, x_ref, a_ref, c_ref, o_ref, p_ref, gmix_ref, wg_ref, wa_ref, wc_ref, wo_ref, wp_ref,
                wout_ref, gffn_ref, wrh_ref, wrm_ref, wrl_ref, br_ref,
                xnew_ref, xn2_ref, ri_ref, rw_ref, cnt_ref, carry):
    i = pl.program_id(0)

    @pl.when(i == 0)
    def _():
        carry[...] = jnp.zeros_like(carry)

    x = x_ref[...]
    xn = _rms(x, gmix_ref[...]).astype(BF16)
    h = None
    for bi, (b_ref, w_ref) in enumerate(((a_ref, wa_ref), (c_ref, wc_ref), (o_ref, wo_ref), (p_ref, wp_ref))):
        gate = _sigmoid(_dot(xn, wg_ref[:, bi * D_MODEL:(bi + 1) * D_MODEL]))
        term = gate * _dot(b_ref[...], w_ref[...])
        h = term if h is None else h + term
    x_new = x + _dot(h.astype(BF16), wout_ref[...])
    xnew_ref[...] = x_new
    xn2 = _rms(x_new, gffn_ref[...])
    _store_row_tiles(xn2_ref, xn2)

    xh, xm, xl = _split3(xn2)
    logits = (_dot(xh, wrh_ref[...]) + (_dot(xh, wrm_ref[...]) + _dot(xm, wrh_ref[...]))
              + (_dot(xh, wrl_ref[...]) + _dot(xl, wrh_ref[...]) + _dot(xm, wrm_ref[...]))) + br_ref[...]
    lane = lax.broadcasted_iota(I32, (tm, LANES), 1).astype(F32)
    neg = -jnp.inf
    lg = jnp.where(lane < N_GROUPS, logits, neg)
    mg = jnp.max(lg, axis=-1, keepdims=True)
    p_grp = 1.0 / jnp.sum(jnp.exp(lg - mg), axis=-1, keepdims=True)
    g_sel = jnp.min(jnp.where(lg == mg, lane, float(LANES)), axis=-1, keepdims=True)
    lo = N_GROUPS + EXPERTS_PER_GROUP * g_sel
    le = jnp.where((lane >= lo) & (lane < lo + EXPERTS_PER_GROUP), logits, neg)
    m1 = jnp.max(le, axis=-1, keepdims=True)
    i1 = jnp.min(jnp.where(le == m1, lane, float(LANES)), axis=-1, keepdims=True)
    le2 = jnp.where(lane == i1, neg, le)
    m2 = jnp.max(le2, axis=-1, keepdims=True)
    i2 = jnp.min(jnp.where(le2 == m2, lane, float(LANES)), axis=-1, keepdims=True)
    e2 = jnp.exp(m2 - m1)
    w1 = p_grp / (1.0 + e2)
    w2 = p_grp * e2 / (1.0 + e2)
    e1 = i1 - N_GROUPS
    e2i = i2 - N_GROUPS

    oh1 = lane == e1
    oh2 = lane == e2i
    oh = jnp.where(oh1, 1.0, jnp.where(oh2, 1.0, 0.0))
    r = lax.broadcasted_iota(I32, (tm, tm), 0)
    c = lax.broadcasted_iota(I32, (tm, tm), 1)
    before = _dot(jnp.where(r > c, 1.0, 0.0).astype(BF16), oh.astype(BF16)) + carry[...]
    rank1 = jnp.sum(jnp.where(oh1, before, 0.0), axis=-1, keepdims=True)
    rank2 = jnp.sum(jnp.where(oh2, before, 0.0), axis=-1, keepdims=True)
    carry[...] = carry[...] + jnp.sum(oh, axis=0, keepdims=True)
    cnt_ref[...] = jnp.broadcast_to(carry[...], cnt_ref.shape)
    ri = jnp.where(lane == 0, e1, jnp.where(lane == 1, e2i, jnp.where(lane == 2, rank1,
                                                                      jnp.where(lane == 3, rank2, 0.0))))
    ri_ref[...] = ri.astype(I32)
    rw_ref[...] = jnp.where(lane == 0, w1, jnp.where(lane == 1, w2, 0.0))


def _merge(x, attn, c, o, pooled, lw):
    T = x.shape[0]
    tm = min(512, T)
    tile = lambda n: pl.BlockSpec((tm, n), lambda i: (i, 0))
    full = lambda a: pl.BlockSpec(a.shape, lambda i: (0,) * a.ndim)
    weights = [lw["norm_mix_g"], lw["w_gate"], lw["w_branch_attn"], lw["w_branch_conf"], lw["w_branch_gdn"],
               lw["w_branch_pool"], lw["w_out"], lw["norm_ffn_g"], lw["w_route_hi"], lw["w_route_mid"],
               lw["w_route_lo"], lw["b_route"]]
    return pl.pallas_call(
        functools.partial(_merge_body, tm),
        grid=(T // tm,),
        in_specs=[tile(D_MODEL), tile(SWA_Q), tile(CONF_DIM), tile(GDN_HEADS * GDN_DV), tile(POOL_DIM)]
                 + [full(w) for w in weights],
        out_specs=[tile(D_MODEL), pl.BlockSpec((tm * ROW_SUB, LANES), lambda i: (i, 0)), tile(LANES), tile(LANES),
                   pl.BlockSpec((8, LANES), lambda i: (0, 0))],
        out_shape=[jax.ShapeDtypeStruct((T, D_MODEL), F32), jax.ShapeDtypeStruct((T * ROW_SUB, LANES), F32),
                   jax.ShapeDtypeStruct((T, LANES), I32), jax.ShapeDtypeStruct((T, LANES), F32),
                   jax.ShapeDtypeStruct((8, LANES), F32)],
        scratch_shapes=[pltpu.VMEM((1, LANES), F32)],
        compiler_params=_params("arbitrary"),
        name="merge_route",
    )(x, attn, c, o, pooled, *weights)


def _row_tile(ref, r):
    return ref.at[pl.ds(pl.multiple_of(r * ROW_SUB, ROW_SUB), ROW_SUB), :]


def _row_copy(src, s_row, dst, d_row, sem):
    return pltpu.make_async_copy(_row_tile(src, s_row), _row_tile(dst, d_row), sem)


def _load_row_tiles(ref, n, first=0):
    return jnp.concatenate([ref[pl.ds(first * ROW_SUB + s, n, stride=ROW_SUB), :] for s in range(ROW_SUB)], axis=1)


def _store_row_tiles(ref, x):
    for s in range(ROW_SUB):
        ref[pl.ds(s, x.shape[0], stride=ROW_SUB), :] = x[:, s * LANES:(s + 1) * LANES]


def _dispatch_body(tc, n_blocks, dest_ref, pend_ref, x_ref, xb_ref, zbuf, sem_z, sem_s):
    i = pl.program_id(0)

    def zero_block(start):
        return pltpu.make_async_copy(zbuf, xb_ref.at[pl.ds(pl.multiple_of(start * ROW_SUB, MOE_BLOCK), MOE_BLOCK * ROW_SUB), :], sem_z)

    def zero_copy(e):
        return zero_block(jnp.maximum(pend_ref[e] - MOE_BLOCK, 0))

    @pl.when(i == 0)
    def _():
        zbuf[...] = jnp.zeros_like(zbuf)
        n_used = pend_ref[N_EXPERTS - 1] // MOE_BLOCK

        @pl.loop(0, N_EXPERTS)
        def _(e):
            zero_copy(e).start()

        @pl.loop(n_used, n_blocks)
        def _(b):
            zero_block(b * MOE_BLOCK).start()

        @pl.loop(0, N_EXPERTS)
        def _(e):
            zero_copy(e).wait()

        @pl.loop(n_used, n_blocks)
        def _(b):
            zero_block(b * MOE_BLOCK).wait()

    def copies(r):
        slot = (i * tc + r) * 2
        return (_row_copy(x_ref, r, xb_ref, dest_ref[slot], sem_s),
                _row_copy(x_ref, r, xb_ref, dest_ref[slot + 1], sem_s))

    @pl.loop(0, tc)
    def _(r):
        for queue, cp in enumerate(copies(r)):
            cp.start(priority=queue)

    @pl.loop(0, tc)
    def _(r):
        for cp in copies(r):
            cp.wait()


def _dispatch(xn2, dest_flat, pad_end, n_rows):
    T = xn2.shape[0] // ROW_SUB
    tc = min(512, T)
    return pl.pallas_call(
        functools.partial(_dispatch_body, tc, n_rows // MOE_BLOCK),
        grid_spec=pltpu.PrefetchScalarGridSpec(
            num_scalar_prefetch=2,
            grid=(T // tc,),
            in_specs=[pl.BlockSpec((tc * ROW_SUB, LANES), lambda i, d, p: (i, 0))],
            out_specs=pl.BlockSpec(memory_space=pl.ANY),
            scratch_shapes=[pltpu.VMEM((MOE_BLOCK * ROW_SUB, LANES), F32), pltpu.SemaphoreType.DMA(()),
                            pltpu.SemaphoreType.DMA(())]),
        out_shape=jax.ShapeDtypeStruct((n_rows * ROW_SUB, LANES), F32),
        compiler_params=_params("arbitrary"),
        name="moe_dispatch",
    )(dest_flat, pad_end, xn2)


def _expert_body(be_ref, nu_ref, x_ref, wg_ref, wu_ref, wd_ref, y_ref, wgu_b, wd_b):
    i = pl.program_id(0)

    @pl.when((i == 0) | (be_ref[i] != be_ref[jnp.maximum(i - 1, 0)]))
    def _():
        wgu_b[:, :D_EXPERT] = wg_ref[...].astype(BF16)
        wgu_b[:, D_EXPERT:] = wu_ref[...].astype(BF16)
        wd_b[...] = wd_ref[...].astype(BF16)

    @pl.when(i < nu_ref[0])
    def _():
        x = _load_row_tiles(x_ref, MOE_BLOCK).astype(BF16)
        au = _dot(x, wgu_b[...])
        a = au[:, :D_EXPERT]
        h = (a * _sigmoid(a)) * au[:, D_EXPERT:]
        _store_row_tiles(y_ref, _dot(h.astype(BF16), wd_b[...]))

    @pl.when(i >= nu_ref[0])
    def _():
        y_ref[...] = jnp.zeros_like(y_ref)


def _experts(xb, block_expert, n_used, lw):
    n_rows = xb.shape[0] // ROW_SUB
    n_blocks = n_rows // MOE_BLOCK
    layer = lw["layer"]
    blk = (MOE_BLOCK * ROW_SUB, LANES)
    return pl.pallas_call(
        _expert_body,
        grid_spec=pltpu.PrefetchScalarGridSpec(
            num_scalar_prefetch=2,
            grid=(n_blocks,),
            in_specs=[pl.BlockSpec(blk, lambda i, be, nu: (jnp.minimum(i, nu[0] - 1), 0)),
                      pl.BlockSpec((None, None, D_MODEL, D_EXPERT), lambda i, be, nu: (layer, be[i], 0, 0)),
                      pl.BlockSpec((None, None, D_MODEL, D_EXPERT), lambda i, be, nu: (layer, be[i], 0, 0)),
                      pl.BlockSpec((None, None, D_EXPERT, D_MODEL), lambda i, be, nu: (layer, be[i], 0, 0))],
            out_specs=pl.BlockSpec(blk, lambda i, be, nu: (i, 0)),
            scratch_shapes=[pltpu.VMEM((D_MODEL, 2 * D_EXPERT), BF16), pltpu.VMEM((D_EXPERT, D_MODEL), BF16)]),
        out_shape=jax.ShapeDtypeStruct((n_rows * ROW_SUB, LANES), F32),
        compiler_params=_params("arbitrary"),
        name="moe_experts",
    )(block_expert, n_used, xb, lw["w_exp_gate"], lw["w_exp_up"], lw["w_exp_down"])


def _combine_body(tc, final, dest_ref, x_ref, w_ref, g_ref, yb_ref, out_ref, ybuf, sem):
    i = pl.program_id(0)

    def copies(step, buf, r):
        tok = step * tc + r
        return (_row_copy(yb_ref, dest_ref[2 * tok], ybuf.at[buf], r, sem.at[buf]),
                _row_copy(yb_ref, dest_ref[2 * tok + 1], ybuf.at[buf], tc + r, sem.at[buf]))

    def start_step(step, buf):
        @pl.loop(0, tc)
        def _(r):
            for queue, cp in enumerate(copies(step, buf, r)):
                cp.start(priority=queue)

    @pl.when(i == 0)
    def _():
        start_step(i, 0)

    for buf in range(2):
        @pl.when(i % 2 == buf)
        def _():
            @pl.when(i + 1 < pl.num_programs(0))
            def _():
                start_step(i + 1, 1 - buf)

            @pl.loop(0, tc)
            def _(r):
                for cp in copies(i, buf, r):
                    cp.wait()

            w = w_ref[...]
            y0 = _load_row_tiles(ybuf.at[buf], tc)
            y1 = _load_row_tiles(ybuf.at[buf], tc, first=tc)
            out = x_ref[...] + (y0 * w[:, 0:1] + y1 * w[:, 1:2])
            if final:
                out = _rms(out, g_ref[...])
            out_ref[...] = out


def _combine(x_new, route_w, yb, dest_flat, g_final, final):
    T = x_new.shape[0]
    tc = min(256, T)
    return pl.pallas_call(
        functools.partial(_combine_body, tc, final),
        grid_spec=pltpu.PrefetchScalarGridSpec(
            num_scalar_prefetch=1,
            grid=(T // tc,),
            in_specs=[pl.BlockSpec((tc, D_MODEL), lambda i, d: (i, 0)),
                      pl.BlockSpec((tc, LANES), lambda i, d: (i, 0)),
                      pl.BlockSpec((1, D_MODEL), lambda i, d: (0, 0)),
                      pl.BlockSpec(memory_space=pl.ANY)],
            out_specs=pl.BlockSpec((tc, D_MODEL), lambda i, d: (i, 0)),
            scratch_shapes=[pltpu.VMEM((2, 2 * tc * ROW_SUB, LANES), F32), pltpu.SemaphoreType.DMA((2,))]),
        out_shape=jax.ShapeDtypeStruct((T, D_MODEL), F32),
        compiler_params=_params("arbitrary"),
        name="moe_combine",
    )(dest_flat, x_new, route_w, g_final, yb)


def _moe(x_new, xn2, route_i, route_w, counts_f, lw, g_final, final):
    T = x_new.shape[0]
    counts = counts_f[0, :N_EXPERTS].astype(I32)
    padded = (counts + MOE_BLOCK - 1) // MOE_BLOCK * MOE_BLOCK
    pad_end = jnp.cumsum(padded)
    pad_start = pad_end - padded
    eid = route_i[:, 0:2]
    onehot = eid[:, :, None] == jnp.arange(N_EXPERTS, dtype=I32)[None, None, :]
    dest = route_i[:, 2:4] + jnp.sum(jnp.where(onehot, pad_start[None, None, :], 0), axis=-1)
    dest_flat = dest.reshape(-1).astype(I32)
    n_blocks = (2 * T + N_EXPERTS * (MOE_BLOCK - 1)) // MOE_BLOCK
    n_used = (pad_end[N_EXPERTS - 1:] // MOE_BLOCK).astype(I32)
    block_start = jnp.arange(n_blocks, dtype=I32) * MOE_BLOCK
    block_expert = jnp.minimum(jnp.sum(block_start[:, None] >= pad_end[None, :], axis=-1), N_EXPERTS - 1).astype(I32)
    xb = _dispatch(xn2, dest_flat, pad_end.astype(I32), n_blocks * MOE_BLOCK)
    yb = _experts(xb, block_expert, n_used, lw)
    return _combine(x_new, route_w, yb, dest_flat, g_final, final)


def _layer_weights(p, l):
    w_in = p["w_in"][l]
    q_w = w_in[:, :SWA_Q].reshape(D_MODEL, SWA_KV_HEADS, SWA_GROUP, HEAD_DIM).transpose(0, 2, 1, 3)
    off = SWA_Q + 2 * SWA_KV
    conf_w = w_in[:, off:off + 2 * CONF_DIM]
    off += 2 * CONF_DIM
    gqkv_w = w_in[:, off:off + 3 * GDN_DIM]
    off += 3 * GDN_DIM
    gz_w = w_in[:, off:off + GDN_HEADS * GDN_DV]
    off += GDN_HEADS * GDN_DV
    gba_w = jnp.pad(w_in[:, off:off + 2 * GDN_HEADS], ((0, 0), (0, LANES - 2 * GDN_HEADS)))
    off += 2 * GDN_HEADS
    pool_w_in = w_in[:, off:off + POOL_DIM]
    off += POOL_DIM
    w_small = jnp.concatenate([q_w.reshape(D_MODEL, SWA_Q), w_in[:, SWA_Q:SWA_Q + 2 * SWA_KV], conf_w, gqkv_w, gz_w,
                               pool_w_in, gba_w], axis=1).astype(BF16)
    pool_bd = jnp.zeros((POOL_DIM, POOL_DIM), F32)
    for gi in range(len(POOL_WINDOWS)):
        sl = slice(gi * POOL_GROUP_DIM, (gi + 1) * POOL_GROUP_DIM)
        pool_bd = pool_bd.at[sl, sl].set(p["pool_w"][l, gi])
    lane_pad = lambda v: jnp.pad(v, (GDN_HEADS, LANES - 2 * GDN_HEADS))[None, :]
    w_route = jnp.pad(jnp.concatenate([p["w_group"][l], p["w_router"][l]], axis=1),
                      ((0, 0), (0, LANES - N_GROUPS - N_EXPERTS)))
    r_hi = w_route.astype(BF16)
    r_mid = (w_route - r_hi.astype(F32)).astype(BF16)
    r_lo = (w_route - r_hi.astype(F32) - r_mid.astype(F32)).astype(BF16)
    return {
        "norm_mix_g": p["norm_mix_g"][l][None, :],
        "w_small": w_small,
        "w_gate": w_in[:, off:].astype(BF16),
        "swa_sinks": p["swa_sinks"][l],
        "conf_w": jnp.pad(p["conf_dw_w"][l], ((0, CONF_HIST - CONF_KERNEL), (0, 0))),
        "conf_b": p["conf_dw_b"][l][None, :],
        "conf_ln_g": p["conf_ln_g"][l][None, :],
        "conf_ln_b": p["conf_ln_b"][l][None, :],
        "gdn_conv_w": jnp.pad(p["gdn_conv_w"][l], ((0, GDN_HIST - GDN_CONV), (0, 0))),
        "gdn_a_log": lane_pad(p["gdn_a_log"][l]),
        "gdn_dt_bias": lane_pad(p["gdn_dt_bias"][l]),
        "gdn_norm_g": jnp.tile(p["gdn_norm_g"][l], 2)[None, :],
        "pool_w": pool_bd.astype(BF16),
        "pool_scale": p["pool_scale"][l][None, :],
        "w_branch_attn": p["w_branch_attn"][l].reshape(SWA_KV_HEADS, SWA_GROUP, HEAD_DIM, D_MODEL)
                         .transpose(1, 0, 2, 3).reshape(SWA_Q, D_MODEL).astype(BF16),
        "w_branch_conf": p["w_branch_conf"][l].astype(BF16),
        "w_branch_gdn": p["w_branch_gdn"][l].astype(BF16),
        "w_branch_pool": p["w_branch_pool"][l].astype(BF16),
        "w_out": p["w_out"][l].astype(BF16),
        "norm_ffn_g": p["norm_ffn_g"][l][None, :],
        "w_route_hi": r_hi, "w_route_mid": r_mid, "w_route_lo": r_lo,
        "b_route": jnp.pad(jnp.concatenate([p["b_group"][l], p["b_router"][l]]),
                           (0, LANES - N_GROUPS - N_EXPERTS))[None, :],
        "layer": l,
        "w_exp_gate": p["w_exp_gate"],
        "w_exp_up": p["w_exp_up"],
        "w_exp_down": p["w_exp_down"],
    }


def _pad_rows(a, n, front=False):
    extra = n - a.shape[1]
    return jnp.pad(a, ((0, 0), (extra, 0) if front else (0, extra), (0, 0)))


def _pair_state(s):
    z = jnp.zeros_like(s[:, 0::2])
    top = jnp.concatenate([s[:, 0::2], z], axis=-1)
    bot = jnp.concatenate([z, s[:, 1::2]], axis=-1)
    return jnp.concatenate([top, bot], axis=-2)


def _unpair_state(sp):
    even = sp[:, :, :GDN_DK, :GDN_DV]
    odd = sp[:, :, GDN_DK:, GDN_DV:]
    return jnp.stack([even, odd], axis=2).reshape(sp.shape[0], GDN_HEADS, GDN_DK, GDN_DV)


def _token_mixer(x, lw, state, start_pos):
    B, L, _ = x.shape
    T = B * L
    q, kv, conf_in, gqkv, gz, pool_in, gba = _proj(x.reshape(T, D_MODEL), lw["norm_mix_g"], lw["w_small"])
    seq = lambda a: a.reshape(B, L, a.shape[-1])
    q, kv, conf_in, gqkv, gz, pool_in, gba = map(seq, (q, kv, conf_in, gqkv, gz, pool_in, gba))
    has_past = state is not None
    if has_past:
        past_k, past_v, conf_buf, gdn_buf, gdn_s, pool_buf = state
        kv0 = jnp.concatenate([past_k.reshape(B, WINDOW, SWA_KV), past_v.reshape(B, WINDOW, SWA_KV)], axis=-1)
        ll, lg = STEP_PAD, CHUNK
    else:
        conf_buf = jnp.zeros((B, CONF_KERNEL - 1, CONF_DIM), F32)
        gdn_buf = jnp.zeros((B, GDN_CONV - 1, 3 * GDN_DIM), F32)
        gdn_s = jnp.zeros((B, GDN_HEADS, GDN_DK, GDN_DV), F32)
        pool_buf = jnp.zeros((B, POOL_MAX - 1, POOL_DIM), F32)
        ll = lg = L
    if has_past:
        attn = _attn_step(_pad_rows(q, STEP_PAD), _pad_rows(kv, STEP_PAD), kv0, lw["swa_sinks"])[:, :L]
    else:
        attn = _attn(q, kv, lw["swa_sinks"])
    c, pooled, cu_last = _local_mixers(_pad_rows(conf_in, ll), _pad_rows(pool_in, ll),
                                       _pad_rows(conf_buf, CONF_HIST, front=True),
                                       _pad_rows(pool_buf, POOL_HIST, front=True), lw, start_pos)
    o, s_fin = _gdn(_pad_rows(gqkv, lg), _pad_rows(gz, lg), _pad_rows(gba, lg),
                    _pad_rows(gdn_buf, GDN_HIST, front=True), _pair_state(gdn_s), lw, L)
    if has_past:
        new_kv = jnp.concatenate([kv0[:, L:], kv], axis=1)
        n_hist = cu_last.shape[1]
        new_conf = jnp.concatenate([conf_buf[:, L:], cu_last[:, n_hist - ll:n_hist - ll + L]], axis=1)
        new_gdn_buf = jnp.concatenate([gdn_buf, gqkv], axis=1)[:, -(GDN_CONV - 1):]
        new_pool = jnp.concatenate([pool_buf[:, L:], pool_in], axis=1)
    else:
        new_kv = kv[:, -WINDOW:]
        new_conf = cu_last[:, -(CONF_KERNEL - 1):]
        new_gdn_buf = gqkv[:, -(GDN_CONV - 1):]
        new_pool = pool_in[:, -(POOL_MAX - 1):]
    new_k = new_kv[..., :SWA_KV].reshape(B, WINDOW, SWA_KV_HEADS, HEAD_DIM)
    new_v = new_kv[..., SWA_KV:].reshape(B, WINDOW, SWA_KV_HEADS, HEAD_DIM)
    flat = lambda a: a[:, :L].reshape(T, a.shape[-1])
    branches = (flat(attn), flat(c), flat(o), flat(pooled))
    return branches, (new_k, new_v, new_conf, new_gdn_buf, _unpair_state(s_fin), new_pool)


def _trunk(x, past, start_pos, layers, g_final):
    B, L, _ = x.shape
    collected = [[] for _ in range(6)]
    xf = x.reshape(B * L, D_MODEL)
    for l, lw in enumerate(layers):
        state = None if past is None else tuple(p[l] for p in past)
        branches, new = _token_mixer(xf.reshape(B, L, D_MODEL), lw, state, start_pos)
        x_new, xn2, route_i, route_w, counts = _merge(xf, *branches, lw)
        xf = _moe(x_new, xn2, route_i, route_w, counts, lw, g_final, l == len(layers) - 1)
        for lst, arr in zip(collected, new):
            lst.append(arr)
    return xf.reshape(B, L, D_MODEL), [jnp.stack(lst, axis=0) for lst in collected]


def kernel(x_prompt, x_sample, cache_swa_k, cache_swa_v, state_conf_conv, state_gdn_conv, state_gdn, state_pool, norm_mix_g, w_in, swa_sinks, conf_dw_w, conf_dw_b, conf_ln_g, conf_ln_b, gdn_conv_w, gdn_a_log, gdn_dt_bias, gdn_norm_g, pool_w, pool_scale, w_branch_attn, w_branch_conf, w_branch_gdn, w_branch_pool, w_out, norm_ffn_g, w_group, b_group, w_router, b_router, w_exp_gate, w_exp_up, w_exp_down, final_norm_g):
    params = dict(norm_mix_g=norm_mix_g, w_in=w_in, swa_sinks=swa_sinks, conf_dw_w=conf_dw_w,
                  conf_dw_b=conf_dw_b, conf_ln_g=conf_ln_g, conf_ln_b=conf_ln_b, gdn_conv_w=gdn_conv_w,
                  gdn_a_log=gdn_a_log, gdn_dt_bias=gdn_dt_bias, gdn_norm_g=gdn_norm_g, pool_w=pool_w,
                  pool_scale=pool_scale, w_branch_attn=w_branch_attn, w_branch_conf=w_branch_conf,
                  w_branch_gdn=w_branch_gdn, w_branch_pool=w_branch_pool, w_out=w_out, norm_ffn_g=norm_ffn_g,
                  w_group=w_group, b_group=b_group, w_router=w_router, b_router=b_router,
                  w_exp_gate=w_exp_gate, w_exp_up=w_exp_up, w_exp_down=w_exp_down)
    layers = [_layer_weights(params, l) for l in range(w_in.shape[0])]
    g_final = final_norm_g[None, :]
    yp, (pk, pv, pc, pgc, pgs, pp) = _trunk(x_prompt, None, 0, layers, g_final)
    past = (cache_swa_k, cache_swa_v, state_conf_conv, state_gdn_conv, state_gdn, state_pool)
    ys, (sk, sv, sc, sgc, sgs, sp) = _trunk(x_sample, past, PAST_LEN, layers, g_final)
    return (yp, ys, pk, pv, pc, pgc, pgs, pp, sk, sv, sc, sgc, sgs, sp)
```

```python
import functools

import jax
import jax.numpy as jnp
from jax import lax
from jax.experimental import pallas as pl
from jax.experimental.pallas import tpu as pltpu

F32 = jnp.float32
BF16 = jnp.bfloat16
I32 = jnp.int32

D_MODEL = 1024
DEPTH = 2
PAST_LEN = 16384
HEAD_DIM = 64
SWA_HEADS = 6
SWA_KV_HEADS = 2
SWA_GROUP = SWA_HEADS // SWA_KV_HEADS
SWA_Q = SWA_HEADS * HEAD_DIM
SWA_KV = SWA_KV_HEADS * HEAD_DIM
WINDOW = 128
ATTN_SCALE = HEAD_DIM ** -0.5
CONF_DIM = D_MODEL // 4
CONF_KERNEL = 31
GDN_HEADS = 4
GDN_DK = 64
GDN_DV = 64
GDN_DIM = GDN_HEADS * GDN_DK
GDN_CONV = 4
POOL_WINDOWS = (2, 4, 8, 16)
POOL_DIM = D_MODEL // 4
POOL_GROUP_DIM = POOL_DIM // len(POOL_WINDOWS)
POOL_MAX = max(POOL_WINDOWS)
N_BRANCHES = 4
N_GROUPS = 4
EXPERTS_PER_GROUP = 8
N_EXPERTS = N_GROUPS * EXPERTS_PER_GROUP
D_EXPERT = D_MODEL // 2
MOE_BLOCK = 256
EPS = 1e-6

LANES = 128
ROW_SUB = D_MODEL // LANES
CHUNK = 128
CONF_HIST = 32
POOL_HIST = 16
GDN_HIST = 8
GDN_ROWS = 4
ATTN_BLOCKS = 4
ATTN_STEP_ROWS = 8
STEP_PAD = 8
VMEM_LIMIT = 56 * 1024 * 1024

PROJ_SPLITS = (SWA_Q, 2 * SWA_KV, 2 * CONF_DIM, 3 * GDN_DIM, GDN_HEADS * GDN_DV, POOL_DIM, LANES)
W_IN_OFFS = (0, SWA_Q, SWA_Q + SWA_KV, SWA_Q + 2 * SWA_KV)


def _params(*sem):
    return pltpu.CompilerParams(dimension_semantics=sem, vmem_limit_bytes=VMEM_LIMIT)


def _rms(x, g):
    return x * lax.rsqrt(jnp.mean(x * x, axis=-1, keepdims=True) + EPS) * g


def _sigmoid(x):
    return 1.0 / (1.0 + jnp.exp(-x))


def _dot(a, b):
    return jnp.dot(a, b, preferred_element_type=F32)


def _dot_nt(a, b):
    return lax.dot_general(a, b, (((1,), (1,)), ((), ())), preferred_element_type=F32)


def _split3(x):
    hi = x.astype(BF16)
    r = x - hi.astype(F32)
    mid = r.astype(BF16)
    lo = (r - mid.astype(F32)).astype(BF16)
    return hi, mid, lo


def _dot_exact_lhs(a_bf16, b):
    hi, mid, lo = _split3(b)
    return _dot(a_bf16, hi) + _dot(a_bf16, mid) + _dot(a_bf16, lo)


def _proj_body(x_ref, g_ref, w_ref, *out_refs):
    xn = _rms(x_ref[...], g_ref[...]).astype(BF16)
    off = 0
    for o_ref, n in zip(out_refs, PROJ_SPLITS):
        o_ref[...] = _dot(xn, w_ref[:, off:off + n])
        off += n


def _proj(x, g, w_small):
    T = x.shape[0]
    tm = min(512, T)
    n_small = sum(PROJ_SPLITS)
    return pl.pallas_call(
        _proj_body,
        grid=(T // tm,),
        in_specs=[pl.BlockSpec((tm, D_MODEL), lambda i: (i, 0)),
                  pl.BlockSpec((1, D_MODEL), lambda i: (0, 0)),
                  pl.BlockSpec((D_MODEL, n_small), lambda i: (0, 0))],
        out_specs=[pl.BlockSpec((tm, n), lambda i: (i, 0)) for n in PROJ_SPLITS],
        out_shape=[jax.ShapeDtypeStruct((T, n), F32) for n in PROJ_SPLITS],
        compiler_params=_params("parallel"),
        name="proj",
    )(x, g, w_small)


def _attn_body(sink_ref, q_ref, kv_ref, kvp_ref, o_ref):
    i = pl.program_id(1)
    nsub = q_ref.shape[0] // WINDOW
    kv = kv_ref[...]
    blocks = [kvp_ref[...]] + [kv[s * WINDOW:(s + 1) * WINDOW] for s in range(nsub)]
    k2 = [jnp.concatenate([blocks[s][:, :LANES], blocks[s + 1][:, :LANES]], axis=0).astype(BF16) for s in range(nsub)]
    v2 = [jnp.concatenate([blocks[s][:, LANES:], blocks[s + 1][:, LANES:]], axis=0).astype(BF16) for s in range(nsub)]
    left = lax.broadcasted_iota(I32, (WINDOW, LANES), 1) < HEAD_DIM
    row = lax.broadcasted_iota(I32, (WINDOW, 2 * WINDOW), 0)
    col = lax.broadcasted_iota(I32, (WINDOW, 2 * WINDOW), 1)
    band = (col > row) & (col <= row + WINDOW)
    first = band & ((col >= WINDOW) | (i > 0))
    items = [(s, g, j) for s in range(nsub) for g in range(SWA_GROUP) for j in range(SWA_KV_HEADS)]
    qj, sc, m, p, denom, pv = {}, {}, {}, {}, {}, {}
    for (s, g, j) in items:
        qg = q_ref[s * WINDOW:(s + 1) * WINDOW, g * LANES:(g + 1) * LANES] * ATTN_SCALE
        qj[s, g, j] = jnp.where(left if j == 0 else jnp.logical_not(left), qg, 0.0).astype(BF16)
    for it in items:
        sc[it] = jnp.where(first if it[0] == 0 else band, _dot_nt(qj[it], k2[it[0]]), -jnp.inf)
    for (s, g, j) in items:
        sink = sink_ref[j * SWA_GROUP + g]
        m[s, g, j] = jnp.maximum(jnp.max(sc[s, g, j], axis=-1, keepdims=True), sink)
        p[s, g, j] = jnp.exp(sc[s, g, j] - m[s, g, j])
        denom[s, g, j] = jnp.sum(p[s, g, j], axis=-1, keepdims=True) + jnp.exp(sink - m[s, g, j])
    for it in items:
        pv[it] = _dot(p[it].astype(BF16), v2[it[0]]) / denom[it]
    for s in range(nsub):
        for g in range(SWA_GROUP):
            og = jnp.where(left, pv[s, g, 0], pv[s, g, 1])
            o_ref[s * WINDOW:(s + 1) * WINDOW, g * LANES:(g + 1) * LANES] = og.astype(BF16)


def _attn(q, kv, sinks):
    B, L, _ = q.shape
    tq = ATTN_BLOCKS * WINDOW
    return pl.pallas_call(
        _attn_body,
        grid=(B, L // tq),
        in_specs=[pl.BlockSpec(memory_space=pltpu.SMEM),
                  pl.BlockSpec((None, tq, SWA_Q), lambda b, i: (b, i, 0)),
                  pl.BlockSpec((None, tq, 2 * SWA_KV), lambda b, i: (b, i, 0)),
                  pl.BlockSpec((None, WINDOW, 2 * SWA_KV), lambda b, i: (b, jnp.maximum(ATTN_BLOCKS * i - 1, 0), 0))],
        out_specs=pl.BlockSpec((None, tq, SWA_Q), lambda b, i: (b, i, 0)),
        out_shape=jax.ShapeDtypeStruct((B, L, SWA_Q), BF16),
        compiler_params=_params("parallel", "arbitrary"),
        name="swa",
    )(sinks, q, kv, kv)


def _attn_step_body(sink_ref, q_ref, kv_ref, kv0_ref, o_ref):
    n_seq, tq, _ = q_ref.shape
    heads = [(g, j) for g in range(SWA_GROUP) for j in range(SWA_KV_HEADS)]
    nq = len(heads) * tq
    row = lax.broadcasted_iota(I32, (nq, LANES), 0)
    col = lax.broadcasted_iota(I32, (nq, LANES), 1)
    tok = lax.rem(row, tq)
    in_cache = col > tok
    causal = col <= tok
    left = lax.broadcasted_iota(I32, (tq, LANES), 1) < HEAD_DIM
    sink = jnp.zeros((nq, LANES), F32)
    for h, (g, j) in enumerate(heads):
        sink = jnp.where(row // tq == h, sink_ref[j * SWA_GROUP + g], sink)
    zpad = jnp.zeros((WINDOW - tq, LANES), F32)
    seqs = range(n_seq)
    q = [q_ref[b] * ATTN_SCALE for b in seqs]
    qs = [jnp.concatenate([jnp.where(left if j == 0 else jnp.logical_not(left), q[b][:, g * LANES:(g + 1) * LANES], 0.0)
                           for (g, j) in heads], axis=0).astype(BF16) for b in seqs]
    kp = [kv0_ref[b, :, :LANES].astype(BF16) for b in seqs]
    vp = [kv0_ref[b, :, LANES:].astype(BF16) for b in seqs]
    kc = [jnp.concatenate([kv_ref[b, :, :LANES], zpad], axis=0).astype(BF16) for b in seqs]
    vc = [jnp.concatenate([kv_ref[b, :, LANES:], zpad], axis=0).astype(BF16) for b in seqs]
    sp = [jnp.where(in_cache, _dot_nt(qs[b], kp[b]), -jnp.inf) for b in seqs]
    sc = [jnp.where(causal, _dot_nt(qs[b], kc[b]), -jnp.inf) for b in seqs]
    m = [jnp.maximum(jnp.maximum(jnp.max(sp[b], axis=-1, keepdims=True), jnp.max(sc[b], axis=-1, keepdims=True)), sink)
         for b in seqs]
    pp = [jnp.exp(sp[b] - m[b]) for b in seqs]
    pc = [jnp.exp(sc[b] - m[b]) for b in seqs]
    denom = [jnp.sum(pp[b], axis=-1, keepdims=True) + jnp.sum(pc[b], axis=-1, keepdims=True) + jnp.exp(sink - m[b])
             for b in seqs]
    pv = [(_dot(pp[b].astype(BF16), vp[b]) + _dot(pc[b].astype(BF16), vc[b])) / denom[b] for b in seqs]
    for b in seqs:
        for g in range(SWA_GROUP):
            r0 = g * SWA_KV_HEADS * tq
            og = jnp.where(left, pv[b][r0:r0 + tq], pv[b][r0 + tq:r0 + 2 * tq])
            o_ref[b, :, g * LANES:(g + 1) * LANES] = og.astype(BF16)


def _attn_step(q, kv, kv0, sinks):
    B, tq, _ = q.shape
    G = ATTN_STEP_ROWS
    return pl.pallas_call(
        _attn_step_body,
        grid=(B // G,),
        in_specs=[pl.BlockSpec(memory_space=pltpu.SMEM),
                  pl.BlockSpec((G, tq, SWA_Q), lambda b: (b, 0, 0)),
                  pl.BlockSpec((G, tq, 2 * SWA_KV), lambda b: (b, 0, 0)),
                  pl.BlockSpec((G, WINDOW, 2 * SWA_KV), lambda b: (b, 0, 0))],
        out_specs=pl.BlockSpec((G, tq, SWA_Q), lambda b: (b, 0, 0)),
        out_shape=jax.ShapeDtypeStruct((B, tq, SWA_Q), BF16),
        compiler_params=_params("parallel"),
        name="swa_step",
    )(sinks, q, kv, kv0)


def _local_body(tl, start_pos, conf_ref, pool_ref, chist_ref, phist_ref, cw_ref, cb_ref, lng_ref, lnb_ref,
                pw_ref, ps_ref, c_ref, pooled_ref, culast_ref, cbuf, pbuf):
    for g in range(conf_ref.shape[0]):
        _local_row(tl, start_pos, conf_ref.at[g], pool_ref.at[g], chist_ref.at[g], phist_ref.at[g], cw_ref, cb_ref,
                   lng_ref, lnb_ref, pw_ref, ps_ref, c_ref.at[g], pooled_ref.at[g], culast_ref.at[g], cbuf.at[g],
                   pbuf.at[g])


def _local_row(tl, start_pos, conf_ref, pool_ref, chist_ref, phist_ref, cw_ref, cb_ref, lng_ref, lnb_ref,
               pw_ref, ps_ref, c_ref, pooled_ref, culast_ref, cbuf, pbuf):
    l = pl.program_id(1)

    @pl.when(l == 0)
    def _():
        cbuf[0:CONF_HIST, :] = chist_ref[...]
        pbuf[0:POOL_HIST, :] = phist_ref[...]

    ci = conf_ref[...]
    cbuf[CONF_HIST:CONF_HIST + tl, :] = ci[:, :CONF_DIM] * _sigmoid(ci[:, CONF_DIM:])
    base = CONF_HIST - (CONF_KERNEL - 1)
    acc = cw_ref[0:1, :] * cbuf[base:base + tl, :]
    for j in range(1, CONF_KERNEL):
        acc = acc + cw_ref[j:j + 1, :] * cbuf[base + j:base + j + tl, :]
    y = acc + cb_ref[...]
    yc = y - jnp.mean(y, axis=-1, keepdims=True)
    yn = yc * lax.rsqrt(jnp.mean(yc * yc, axis=-1, keepdims=True) + EPS) * lng_ref[...] + lnb_ref[...]
    c_ref[...] = (yn * _sigmoid(yn)).astype(BF16)
    tail = cbuf[tl:tl + CONF_HIST, :]
    culast_ref[...] = tail
    cbuf[0:CONF_HIST, :] = tail

    u = pool_ref[...]
    pbuf[POOL_HIST:POOL_HIST + tl, :] = u
    lane = lax.broadcasted_iota(I32, (tl, POOL_DIM), 1)
    n_avail = (start_pos + 1 + l * tl + lax.broadcasted_iota(I32, (tl, POOL_DIM), 0)).astype(F32)
    wsum = u
    mean = None
    shift = 1
    for gi, wlen in enumerate(POOL_WINDOWS):
        while shift < wlen:
            wsum = wsum + pbuf[POOL_HIST - shift:POOL_HIST - shift + tl, :]
            shift += 1
        m = wsum / jnp.minimum(n_avail, float(wlen))
        mean = m if mean is None else jnp.where(lane >= gi * POOL_GROUP_DIM, m, mean)
    mixed = (mean - u).astype(BF16)
    pooled_ref[...] = (_dot(mixed, pw_ref[...]) * ps_ref[...]).astype(BF16)
    pbuf[0:POOL_HIST, :] = pbuf[tl:tl + POOL_HIST, :]


def _local_mixers(conf_in, pool_in, conf_hist, pool_hist, lw, start_pos):
    B, L, _ = conf_in.shape
    tl = min(512, L)
    G = 1 if L > STEP_PAD else ATTN_STEP_ROWS
    row = lambda n: pl.BlockSpec((1, n), lambda b, l: (0, 0))
    return pl.pallas_call(
        functools.partial(_local_body, tl, start_pos),
        grid=(B // G, L // tl),
        in_specs=[pl.BlockSpec((G, tl, 2 * CONF_DIM), lambda b, l: (b, l, 0)),
                  pl.BlockSpec((G, tl, POOL_DIM), lambda b, l: (b, l, 0)),
                  pl.BlockSpec((G, CONF_HIST, CONF_DIM), lambda b, l: (b, 0, 0)),
                  pl.BlockSpec((G, POOL_HIST, POOL_DIM), lambda b, l: (b, 0, 0)),
                  pl.BlockSpec((CONF_HIST, CONF_DIM), lambda b, l: (0, 0)),
                  row(CONF_DIM), row(CONF_DIM), row(CONF_DIM),
                  pl.BlockSpec((POOL_DIM, POOL_DIM), lambda b, l: (0, 0)),
                  row(POOL_DIM)],
        out_specs=[pl.BlockSpec((G, tl, CONF_DIM), lambda b, l: (b, l, 0)),
                   pl.BlockSpec((G, tl, POOL_DIM), lambda b, l: (b, l, 0)),
                   pl.BlockSpec((G, CONF_HIST, CONF_DIM), lambda b, l: (b, 0, 0))],
        out_shape=[jax.ShapeDtypeStruct((B, L, CONF_DIM), BF16),
                   jax.ShapeDtypeStruct((B, L, POOL_DIM), BF16),
                   jax.ShapeDtypeStruct((B, CONF_HIST, CONF_DIM), F32)],
        scratch_shapes=[pltpu.VMEM((G, tl + CONF_HIST, CONF_DIM), F32),
                        pltpu.VMEM((G, tl + POOL_HIST, POOL_DIM), F32)],
        compiler_params=_params("parallel", "arbitrary"),
        name="conv_pool",
    )(conf_in, pool_in, conf_hist, pool_hist, lw["conf_w"], lw["conf_b"], lw["conf_ln_g"], lw["conf_ln_b"],
      lw["pool_w"], lw["pool_scale"])


def _unit_lower_inverses(mats, order):
    row = lax.broadcasted_iota(I32, (CHUNK, CHUNK), 0)
    col = lax.broadcasted_iota(I32, (CHUNK, CHUNK), 1)
    eye = jnp.where(row == col, 1.0, 0.0)
    ps = [-a for a in mats]
    ts = [eye + n for n in ps]
    for _ in range(max((order - 1).bit_length() - 1, 0)):
        pbs = [p.astype(BF16) for p in ps]
        ps = [_dot(pb, pb) for pb in pbs]
        ts = [t + _dot(t.astype(BF16), p.astype(BF16)) for t, p in zip(ts, ps)]
    splits = [(_split3(t), _split3(eye + a)) for t, a in zip(ts, mats)]
    resids = [eye - (_dot(mh, th) + (_dot(mh, tm) + _dot(mm, th))) for (th, tm, _), (mh, mm, _) in splits]
    return [t + _dot(sp[0][0], r.astype(BF16)) for t, sp, r in zip(ts, splits, resids)]


def _gdn_body(n_valid, seq_len, gqkv_ref, gz_ref, gba_ref, hist_ref, s0_ref, cw_ref, alog_ref, dtb_ref, ng_ref,
              o_ref, sfin_ref, cbuf, s_scr):
    l = pl.program_id(1)
    rows = range(GDN_ROWS)
    pairs = [(g, p) for g in rows for p in range(GDN_HEADS // 2)]
    heads = [(g, p, hh) for (g, p) in pairs for hh in range(2)]

    @pl.when(l == 0)
    def _():
        cbuf[:, 0:GDN_HIST, :] = hist_ref[...]
        s_scr[...] = s0_ref[...]

    row = lax.broadcasted_iota(I32, (CHUNK, LANES), 0)
    col = lax.broadcasted_iota(I32, (CHUNK, LANES), 1)
    left = col < GDN_DK
    right = jnp.logical_not(left)
    lower = row >= col
    strict = row > col
    blockdiag = (row < GDN_DK) == left

    def halfsum(x):
        sl = jnp.sum(jnp.where(left, x, 0.0), axis=-1, keepdims=True)
        sr = jnp.sum(jnp.where(left, 0.0, x), axis=-1, keepdims=True)
        return jnp.where(left, sl, sr)

    def pair_cols(arr, lane0):
        return jnp.where(left, arr[:, lane0:lane0 + 1], arr[:, lane0 + 1:lane0 + 2])

    base = GDN_HIST - (GDN_CONV - 1)
    qkv = {}
    for g in rows:
        cbuf[g, GDN_HIST:GDN_HIST + CHUNK, :] = gqkv_ref[g]
        acc = cw_ref[0:1, :] * cbuf[g, base:base + CHUNK, :]
        for j in range(1, GDN_CONV):
            acc = acc + cw_ref[j:j + 1, :] * cbuf[g, base + j:base + j + CHUNK, :]
        cbuf[g, 0:GDN_HIST, :] = cbuf[g, CHUNK:CHUNK + GDN_HIST, :]
        qkv[g] = acc * _sigmoid(acc)

    ltri = jnp.where(lower, 1.0, 0.0).astype(BF16)
    beta_all, g_parts = {}, {}
    for g in rows:
        gba = gba_ref[g]
        beta = _sigmoid(gba)
        ga = gba + dtb_ref[...]
        softplus = jnp.maximum(ga, 0.0) + jnp.log(1.0 + jnp.exp(-jnp.abs(ga)))
        gl = -jnp.exp(alog_ref[...]) * softplus
        if n_valid < seq_len:
            live = (l * CHUNK + row) < n_valid
            beta = jnp.where(live, beta, 0.0)
            gl = jnp.where(live, gl, 0.0)
        beta_all[g] = beta
        g_parts[g] = _split3(gl)
    gc_all = {g: _dot(ltri, g_parts[g][0]) + _dot(ltri, g_parts[g][1]) + _dot(ltri, g_parts[g][2]) for g in rows}
    gc_t = {g: gc_all[g].T for g in rows}
    g_last = {g: gc_all[g][CHUNK - 1:CHUNK, :] for g in rows}
    e_gc = {g: jnp.exp(gc_all[g]) for g in rows}
    e_rest = {g: jnp.exp(g_last[g] - gc_all[g]) for g in rows}
    e_tot = {g: jnp.exp(g_last[g]) for g in rows}

    qn, kn, kn_b, kb, rhs, egc_p, erest_p = {}, {}, {}, {}, {}, {}, {}
    for (g, p) in pairs:
        h0 = 2 * p
        qp = qkv[g][:, p * LANES:(p + 1) * LANES]
        kp = qkv[g][:, GDN_DIM + p * LANES:GDN_DIM + (p + 1) * LANES]
        vp = qkv[g][:, 2 * GDN_DIM + p * LANES:2 * GDN_DIM + (p + 1) * LANES]
        qn[g, p] = qp * lax.rsqrt(halfsum(qp * qp) + EPS) * (GDN_DK ** -0.5)
        kn[g, p] = kp * lax.rsqrt(halfsum(kp * kp) + EPS)
        beta_p = pair_cols(beta_all[g], h0)
        egc_p[g, p] = pair_cols(e_gc[g], GDN_HEADS + h0)
        erest_p[g, p] = pair_cols(e_rest[g], GDN_HEADS + h0)
        kb[g, p] = kn[g, p] * beta_p
        rhs[g, p] = jnp.concatenate([vp * beta_p, kb[g, p] * egc_p[g, p]], axis=1).astype(BF16)
        kn_b[g, p] = kn[g, p].astype(BF16)

    decay, kk, qk = {}, {}, {}
    for (g, p, hh) in heads:
        lane = GDN_HEADS + 2 * p + hh
        hm = left if hh == 0 else right
        diff = gc_all[g][:, lane:lane + 1] - gc_t[g][lane:lane + 1, :]
        decay[g, p, hh] = jnp.where(lower, jnp.exp(jnp.where(lower, diff, 0.0)), 0.0)
        kk[g, p, hh] = _dot_nt(jnp.where(hm, kb[g, p], 0.0).astype(BF16), kn_b[g, p])
        qk[g, p, hh] = _dot_nt(jnp.where(hm, qn[g, p], 0.0).astype(BF16), kn_b[g, p])
    t_inv = _unit_lower_inverses([jnp.where(strict, kk[k] * decay[k], 0.0) for k in heads], min(CHUNK, n_valid))
    res = {k: _dot(t.astype(BF16), rhs[k[0], k[1]]) for k, t in zip(heads, t_inv)}
    qkb = {k: (qk[k] * decay[k]).astype(BF16) for k in heads}

    u = {k: jnp.where(left, res[k + (0,)][:, :LANES], res[k + (1,)][:, :LANES]) for k in pairs}
    w = {k: jnp.where(left, res[k + (0,)][:, LANES:], res[k + (1,)][:, LANES:]) for k in pairs}
    s = {k: s_scr[k[0], k[1]] for k in pairs}
    sb = {k: s[k].astype(BF16) for k in pairs}
    v_new = {k: u[k] - _dot(w[k].astype(BF16), sb[k]) for k in pairs}
    vnb = {k: v_new[k].astype(BF16) for k in pairs}
    o = {k: _dot((qn[k] * egc_p[k]).astype(BF16), sb[k])
         + jnp.where(left, _dot(qkb[k + (0,)], vnb[k]), _dot(qkb[k + (1,)], vnb[k])) for k in pairs}
    upd = {k: _dot((kn[k] * erest_p[k]).T.astype(BF16), vnb[k]) for k in pairs}
    for (g, p) in pairs:
        lane = GDN_HEADS + 2 * p
        gt = jnp.where(row < GDN_DK, e_tot[g][:, lane:lane + 1], e_tot[g][:, lane + 1:lane + 2])
        s_scr[g, p] = s[g, p] * gt + jnp.where(blockdiag, upd[g, p], 0.0)
        ok = o[g, p]
        on = ok * lax.rsqrt(halfsum(ok * ok) * (1.0 / GDN_DV) + EPS) * ng_ref[...]
        z = gz_ref[g, :, p * LANES:(p + 1) * LANES]
        o_ref[g, :, p * LANES:(p + 1) * LANES] = (on * (z * _sigmoid(z))).astype(BF16)
    sfin_ref[...] = s_scr[...]


def _gdn(gqkv, gz, gba, hist, s0, lw, n_valid):
    B, L, _ = gqkv.shape
    G = GDN_ROWS
    row = lambda n: pl.BlockSpec((1, n), lambda b, l: (0, 0))
    return pl.pallas_call(
        functools.partial(_gdn_body, n_valid, L),
        grid=(B // G, L // CHUNK),
        in_specs=[pl.BlockSpec((G, CHUNK, 3 * GDN_DIM), lambda b, l: (b, l, 0)),
                  pl.BlockSpec((G, CHUNK, GDN_HEADS * GDN_DV), lambda b, l: (b, l, 0)),
                  pl.BlockSpec((G, CHUNK, LANES), lambda b, l: (b, l, 0)),
                  pl.BlockSpec((G, GDN_HIST, 3 * GDN_DIM), lambda b, l: (b, 0, 0)),
                  pl.BlockSpec((G, 2, LANES, LANES), lambda b, l: (b, 0, 0, 0)),
                  pl.BlockSpec((GDN_HIST, 3 * GDN_DIM), lambda b, l: (0, 0)),
                  row(LANES), row(LANES), row(LANES)],
        out_specs=[pl.BlockSpec((G, CHUNK, GDN_HEADS * GDN_DV), lambda b, l: (b, l, 0)),
                   pl.BlockSpec((G, 2, LANES, LANES), lambda b, l: (b, 0, 0, 0))],
        out_shape=[jax.ShapeDtypeStruct((B, L, GDN_HEADS * GDN_DV), BF16),
                   jax.ShapeDtypeStruct((B, 2, LANES, LANES), F32)],
        scratch_shapes=[pltpu.VMEM((G, CHUNK + GDN_HIST, 3 * GDN_DIM), F32),
                        pltpu.VMEM((G, 2, LANES, LANES), F32)],
        compiler_params=_params("parallel", "arbitrary"),
        name="gdn",
    )(gqkv, gz, gba, hist, s0, lw["gdn_conv_w"], lw["gdn_a_log"], lw["gdn_dt_bias"], lw["gdn_norm_g"])


def _merge_body(tm, x_ref, a_ref, c_ref, o_ref, p_ref, gmix_ref, wg_ref, wa_ref, wc_ref, wo_ref, wp_ref,
                wout_ref, gffn_ref, wr_ref, br_ref,
                xnew_ref, xn2_ref, ri_ref, rw_ref, cnt_ref, carry):
    i = pl.program_id(0)

    @pl.when(i == 0)
    def _():
        carry[...] = jnp.zeros_like(carry)

    x = x_ref[...]
    xn = _rms(x, gmix_ref[...]).astype(BF16)
    h = None
    for bi, (b_ref, w_ref) in enumerate(((a_ref, wa_ref), (c_ref, wc_ref), (o_ref, wo_ref), (p_ref, wp_ref))):
        gate = _sigmoid(_dot(xn, wg_ref[:, bi * D_MODEL:(bi + 1) * D_MODEL]))
        term = gate * _dot(b_ref[...], w_ref[...])
        h = term if h is None else h + term
    x_new = x + _dot(h.astype(BF16), wout_ref[...])
    xnew_ref[...] = x_new
    xn2 = _rms(x_new, gffn_ref[...])
    _store_row_tiles(xn2_ref, xn2)

    xh, xm, _ = _split3(xn2)
    both = _dot(xh, wr_ref[...])
    logits = both[:, :LANES] + (both[:, LANES:] + _dot(xm, wr_ref[:, :LANES])) + br_ref[...]
    lane = lax.broadcasted_iota(I32, (tm, LANES), 1).astype(F32)
    neg = -jnp.inf
    lg = jnp.where(lane < N_GROUPS, logits, neg)
    mg = jnp.max(lg, axis=-1, keepdims=True)
    p_grp = 1.0 / jnp.sum(jnp.exp(lg - mg), axis=-1, keepdims=True)
    g_sel = jnp.min(jnp.where(lg == mg, lane, float(LANES)), axis=-1, keepdims=True)
    lo = N_GROUPS + EXPERTS_PER_GROUP * g_sel
    le = jnp.where((lane >= lo) & (lane < lo + EXPERTS_PER_GROUP), logits, neg)
    m1 = jnp.max(le, axis=-1, keepdims=True)
    i1 = jnp.min(jnp.where(le == m1, lane, float(LANES)), axis=-1, keepdims=True)
    le2 = jnp.where(lane == i1, neg, le)
    m2 = jnp.max(le2, axis=-1, keepdims=True)
    i2 = jnp.min(jnp.where(le2 == m2, lane, float(LANES)), axis=-1, keepdims=True)
    e2 = jnp.exp(m2 - m1)
    w1 = p_grp / (1.0 + e2)
    w2 = p_grp * e2 / (1.0 + e2)
    e1 = i1 - N_GROUPS
    e2i = i2 - N_GROUPS

    oh1 = lane == e1
    oh2 = lane == e2i
    oh = jnp.where(oh1, 1.0, jnp.where(oh2, 1.0, 0.0))
    r = lax.broadcasted_iota(I32, (tm, tm), 0)
    c = lax.broadcasted_iota(I32, (tm, tm), 1)
    before = _dot(jnp.where(r > c, 1.0, 0.0).astype(BF16), oh.astype(BF16)) + carry[...]
    rank1 = jnp.sum(jnp.where(oh1, before, 0.0), axis=-1, keepdims=True)
    rank2 = jnp.sum(jnp.where(oh2, before, 0.0), axis=-1, keepdims=True)
    carry[...] = carry[...] + jnp.sum(oh, axis=0, keepdims=True)
    cnt_ref[...] = jnp.broadcast_to(carry[...], cnt_ref.shape)
    ri = jnp.where(lane == 0, e1, jnp.where(lane == 1, e2i, jnp.where(lane == 2, rank1,
                                                                      jnp.where(lane == 3, rank2, 0.0))))
    ri_ref[...] = ri.astype(I32)
    rw_ref[...] = jnp.where(lane == 0, w1, jnp.where(lane == 1, w2, 0.0))


def _merge(x, attn, c, o, pooled, lw):
    T = x.shape[0]
    tm = min(512, T)
    tile = lambda n: pl.BlockSpec((tm, n), lambda i: (i, 0))
    full = lambda a: pl.BlockSpec(a.shape, lambda i: (0,) * a.ndim)
    weights = [lw["norm_mix_g"], lw["w_gate"], lw["w_branch_attn"], lw["w_branch_conf"], lw["w_branch_gdn"],
               lw["w_branch_pool"], lw["w_out"], lw["norm_ffn_g"], lw["w_route"], lw["b_route"]]
    return pl.pallas_call(
        functools.partial(_merge_body, tm),
        grid=(T // tm,),
        in_specs=[tile(D_MODEL), tile(SWA_Q), tile(CONF_DIM), tile(GDN_HEADS * GDN_DV), tile(POOL_DIM)]
                 + [full(w) for w in weights],
        out_specs=[tile(D_MODEL), pl.BlockSpec((tm * ROW_SUB, LANES), lambda i: (i, 0)), tile(LANES), tile(LANES),
                   pl.BlockSpec((8, LANES), lambda i: (0, 0))],
        out_shape=[jax.ShapeDtypeStruct((T, D_MODEL), F32), jax.ShapeDtypeStruct((T * ROW_SUB, LANES), F32),
                   jax.ShapeDtypeStruct((T, LANES), I32), jax.ShapeDtypeStruct((T, LANES), F32),
                   jax.ShapeDtypeStruct((8, LANES), F32)],
        scratch_shapes=[pltpu.VMEM((1, LANES), F32)],
        compiler_params=_params("arbitrary"),
        name="merge_route",
    )(x, attn, c, o, pooled, *weights)


def _row_tile(ref, r):
    return ref.at[pl.ds(pl.multiple_of(r * ROW_SUB, ROW_SUB), ROW_SUB), :]


def _row_copy(src, s_row, dst, d_row, sem):
    return pltpu.make_async_copy(_row_tile(src, s_row), _row_tile(dst, d_row), sem)


def _load_row_tiles(ref, n, first=0):
    return jnp.concatenate([ref[pl.ds(first * ROW_SUB + s, n, stride=ROW_SUB), :] for s in range(ROW_SUB)], axis=1)


def _store_row_tiles(ref, x):
    for s in range(ROW_SUB):
        ref[pl.ds(s, x.shape[0], stride=ROW_SUB), :] = x[:, s * LANES:(s + 1) * LANES]


def _dispatch_body(tc, n_blocks, dest_ref, pend_ref, x_ref, xb_ref, zbuf, sem_z, sem_s):
    i = pl.program_id(0)

    def zero_block(start):
        return pltpu.make_async_copy(zbuf, xb_ref.at[pl.ds(pl.multiple_of(start * ROW_SUB, MOE_BLOCK), MOE_BLOCK * ROW_SUB), :], sem_z)

    def zero_copy(e):
        return zero_block(jnp.maximum(pend_ref[e] - MOE_BLOCK, 0))

    @pl.when(i == 0)
    def _():
        zbuf[...] = jnp.zeros_like(zbuf)
        n_used = pend_ref[N_EXPERTS - 1] // MOE_BLOCK

        @pl.loop(0, N_EXPERTS)
        def _(e):
            zero_copy(e).start()

        @pl.loop(n_used, n_blocks)
        def _(b):
            zero_block(b * MOE_BLOCK).start()

        @pl.loop(0, N_EXPERTS)
        def _(e):
            zero_copy(e).wait()

        @pl.loop(n_used, n_blocks)
        def _(b):
            zero_block(b * MOE_BLOCK).wait()

    def copies(r):
        slot = (i * tc + r) * 2
        return (_row_copy(x_ref, r, xb_ref, dest_ref[slot], sem_s),
                _row_copy(x_ref, r, xb_ref, dest_ref[slot + 1], sem_s))

    @pl.loop(0, tc)
    def _(r):
        for queue, cp in enumerate(copies(r)):
            cp.start(priority=queue)

    @pl.loop(0, tc)
    def _(r):
        for cp in copies(r):
            cp.wait()


def _dispatch(xn2, dest_flat, pad_end, n_rows):
    T = xn2.shape[0] // ROW_SUB
    tc = min(512, T)
    return pl.pallas_call(
        functools.partial(_dispatch_body, tc, n_rows // MOE_BLOCK),
        grid_spec=pltpu.PrefetchScalarGridSpec(
            num_scalar_prefetch=2,
            grid=(T // tc,),
            in_specs=[pl.BlockSpec((tc * ROW_SUB, LANES), lambda i, d, p: (i, 0))],
            out_specs=pl.BlockSpec(memory_space=pl.ANY),
            scratch_shapes=[pltpu.VMEM((MOE_BLOCK * ROW_SUB, LANES), F32), pltpu.SemaphoreType.DMA(()),
                            pltpu.SemaphoreType.DMA(())]),
        out_shape=jax.ShapeDtypeStruct((n_rows * ROW_SUB, LANES), F32),
        compiler_params=_params("arbitrary"),
        name="moe_dispatch",
    )(dest_flat, pad_end, xn2)


def _expert_body(be_ref, nu_ref, x_ref, wg_ref, wu_ref, wd_ref, y_ref, wgu_b, wd_b):
    i = pl.program_id(0)

    @pl.when((i == 0) | (be_ref[i] != be_ref[jnp.maximum(i - 1, 0)]))
    def _():
        wgu_b[:, :D_EXPERT] = wg_ref[...].astype(BF16)
        wgu_b[:, D_EXPERT:] = wu_ref[...].astype(BF16)
        wd_b[...] = wd_ref[...].astype(BF16)

    @pl.when(i < nu_ref[0])
    def _():
        x = _load_row_tiles(x_ref, MOE_BLOCK).astype(BF16)
        au = _dot(x, wgu_b[...])
        a = au[:, :D_EXPERT]
        h = (a * _sigmoid(a)) * au[:, D_EXPERT:]
        _store_row_tiles(y_ref, _dot(h.astype(BF16), wd_b[...]))

    @pl.when(i >= nu_ref[0])
    def _():
        y_ref[...] = jnp.zeros_like(y_ref)


def _experts(xb, block_expert, n_used, lw):
    n_rows = xb.shape[0] // ROW_SUB
    n_blocks = n_rows // MOE_BLOCK
    layer = lw["layer"]
    blk = (MOE_BLOCK * ROW_SUB, LANES)
    return pl.pallas_call(
        _expert_body,
        grid_spec=pltpu.PrefetchScalarGridSpec(
            num_scalar_prefetch=2,
            grid=(n_blocks,),
            in_specs=[pl.BlockSpec(blk, lambda i, be, nu: (jnp.minimum(i, nu[0] - 1), 0)),
                      pl.BlockSpec((None, None, D_MODEL, D_EXPERT), lambda i, be, nu: (layer, be[i], 0, 0)),
                      pl.BlockSpec((None, None, D_MODEL, D_EXPERT), lambda i, be, nu: (layer, be[i], 0, 0)),
                      pl.BlockSpec((None, None, D_EXPERT, D_MODEL), lambda i, be, nu: (layer, be[i], 0, 0))],
            out_specs=pl.BlockSpec(blk, lambda i, be, nu: (i, 0)),
            scratch_shapes=[pltpu.VMEM((D_MODEL, 2 * D_EXPERT), BF16), pltpu.VMEM((D_EXPERT, D_MODEL), BF16)]),
        out_shape=jax.ShapeDtypeStruct((n_rows * ROW_SUB, LANES), F32),
        compiler_params=_params("arbitrary"),
        name="moe_experts",
    )(block_expert, n_used, xb, lw["w_exp_gate"], lw["w_exp_up"], lw["w_exp_down"])


def _combine_body(tc, final, dest_ref, x_ref, w_ref, g_ref, yb_ref, out_ref, ybuf, sem):
    i = pl.program_id(0)

    def copies(step, buf, r):
        tok = step * tc + r
        return (_row_copy(yb_ref, dest_ref[2 * tok], ybuf.at[buf], r, sem.at[buf]),
                _row_copy(yb_ref, dest_ref[2 * tok + 1], ybuf.at[buf], tc + r, sem.at[buf]))

    def start_step(step, buf):
        @pl.loop(0, tc)
        def _(r):
            for queue, cp in enumerate(copies(step, buf, r)):
                cp.start(priority=queue)

    @pl.when(i == 0)
    def _():
        start_step(i, 0)

    for buf in range(2):
        @pl.when(i % 2 == buf)
        def _():
            @pl.when(i + 1 < pl.num_programs(0))
            def _():
                start_step(i + 1, 1 - buf)

            @pl.loop(0, tc)
            def _(r):
                for cp in copies(i, buf, r):
                    cp.wait()

            w = w_ref[...]
            y0 = _load_row_tiles(ybuf.at[buf], tc)
            y1 = _load_row_tiles(ybuf.at[buf], tc, first=tc)
            out = x_ref[...] + (y0 * w[:, 0:1] + y1 * w[:, 1:2])
            if final:
                out = _rms(out, g_ref[...])
            out_ref[...] = out


def _combine(x_new, route_w, yb, dest_flat, g_final, final):
    T = x_new.shape[0]
    tc = min(256, T)
    return pl.pallas_call(
        functools.partial(_combine_body, tc, final),
        grid_spec=pltpu.PrefetchScalarGridSpec(
            num_scalar_prefetch=1,
            grid=(T // tc,),
            in_specs=[pl.BlockSpec((tc, D_MODEL), lambda i, d: (i, 0)),
                      pl.BlockSpec((tc, LANES), lambda i, d: (i, 0)),
                      pl.BlockSpec((1, D_MODEL), lambda i, d: (0, 0)),
                      pl.BlockSpec(memory_space=pl.ANY)],
            out_specs=pl.BlockSpec((tc, D_MODEL), lambda i, d: (i, 0)),
            scratch_shapes=[pltpu.VMEM((2, 2 * tc * ROW_SUB, LANES), F32), pltpu.SemaphoreType.DMA((2,))]),
        out_shape=jax.ShapeDtypeStruct((T, D_MODEL), F32),
        compiler_params=_params("arbitrary"),
        name="moe_combine",
    )(dest_flat, x_new, route_w, g_final, yb)


def _moe(x_new, xn2, route_i, route_w, counts_f, lw, g_final, final):
    T = x_new.shape[0]
    counts = counts_f[0, :N_EXPERTS].astype(I32)
    padded = (counts + MOE_BLOCK - 1) // MOE_BLOCK * MOE_BLOCK
    pad_end = jnp.cumsum(padded)
    pad_start = pad_end - padded
    eid = route_i[:, 0:2]
    onehot = eid[:, :, None] == jnp.arange(N_EXPERTS, dtype=I32)[None, None, :]
    dest = route_i[:, 2:4] + jnp.sum(jnp.where(onehot, pad_start[None, None, :], 0), axis=-1)
    dest_flat = dest.reshape(-1).astype(I32)
    n_blocks = (2 * T + N_EXPERTS * (MOE_BLOCK - 1)) // MOE_BLOCK
    n_used = (pad_end[N_EXPERTS - 1:] // MOE_BLOCK).astype(I32)
    block_start = jnp.arange(n_blocks, dtype=I32) * MOE_BLOCK
    block_expert = jnp.minimum(jnp.sum(block_start[:, None] >= pad_end[None, :], axis=-1), N_EXPERTS - 1).astype(I32)
    xb = _dispatch(xn2, dest_flat, pad_end.astype(I32), n_blocks * MOE_BLOCK)
    yb = _experts(xb, block_expert, n_used, lw)
    return _combine(x_new, route_w, yb, dest_flat, g_final, final)


def _layer_weights(p, l):
    w_in = p["w_in"][l]
    q_w = w_in[:, :SWA_Q].reshape(D_MODEL, SWA_KV_HEADS, SWA_GROUP, HEAD_DIM).transpose(0, 2, 1, 3)
    off = SWA_Q + 2 * SWA_KV
    conf_w = w_in[:, off:off + 2 * CONF_DIM]
    off += 2 * CONF_DIM
    gqkv_w = w_in[:, off:off + 3 * GDN_DIM]
    off += 3 * GDN_DIM
    gz_w = w_in[:, off:off + GDN_HEADS * GDN_DV]
    off += GDN_HEADS * GDN_DV
    gba_w = jnp.pad(w_in[:, off:off + 2 * GDN_HEADS], ((0, 0), (0, LANES - 2 * GDN_HEADS)))
    off += 2 * GDN_HEADS
    pool_w_in = w_in[:, off:off + POOL_DIM]
    off += POOL_DIM
    w_small = jnp.concatenate([q_w.reshape(D_MODEL, SWA_Q), w_in[:, SWA_Q:SWA_Q + 2 * SWA_KV], conf_w, gqkv_w, gz_w,
                               pool_w_in, gba_w], axis=1).astype(BF16)
    pool_bd = jnp.zeros((POOL_DIM, POOL_DIM), F32)
    for gi in range(len(POOL_WINDOWS)):
        sl = slice(gi * POOL_GROUP_DIM, (gi + 1) * POOL_GROUP_DIM)
        pool_bd = pool_bd.at[sl, sl].set(p["pool_w"][l, gi])
    lane_pad = lambda v: jnp.pad(v, (GDN_HEADS, LANES - 2 * GDN_HEADS))[None, :]
    w_route = jnp.pad(jnp.concatenate([p["w_group"][l], p["w_router"][l]], axis=1),
                      ((0, 0), (0, LANES - N_GROUPS - N_EXPERTS)))
    r_hi = w_route.astype(BF16)
    r_mid = (w_route - r_hi.astype(F32)).astype(BF16)
    return {
        "norm_mix_g": p["norm_mix_g"][l][None, :],
        "w_small": w_small,
        "w_gate": w_in[:, off:].astype(BF16),
        "swa_sinks": p["swa_sinks"][l],
        "conf_w": jnp.pad(p["conf_dw_w"][l], ((0, CONF_HIST - CONF_KERNEL), (0, 0))),
        "conf_b": p["conf_dw_b"][l][None, :],
        "conf_ln_g": p["conf_ln_g"][l][None, :],
        "conf_ln_b": p["conf_ln_b"][l][None, :],
        "gdn_conv_w": jnp.pad(p["gdn_conv_w"][l], ((0, GDN_HIST - GDN_CONV), (0, 0))),
        "gdn_a_log": lane_pad(p["gdn_a_log"][l]),
        "gdn_dt_bias": lane_pad(p["gdn_dt_bias"][l]),
        "gdn_norm_g": jnp.tile(p["gdn_norm_g"][l], 2)[None, :],
        "pool_w": pool_bd.astype(BF16),
        "pool_scale": p["pool_scale"][l][None, :],
        "w_branch_attn": p["w_branch_attn"][l].reshape(SWA_KV_HEADS, SWA_GROUP, HEAD_DIM, D_MODEL)
                         .transpose(1, 0, 2, 3).reshape(SWA_Q, D_MODEL).astype(BF16),
        "w_branch_conf": p["w_branch_conf"][l].astype(BF16),
        "w_branch_gdn": p["w_branch_gdn"][l].astype(BF16),
        "w_branch_pool": p["w_branch_pool"][l].astype(BF16),
        "w_out": p["w_out"][l].astype(BF16),
        "norm_ffn_g": p["norm_ffn_g"][l][None, :],
        "w_route": jnp.concatenate([r_hi, r_mid], axis=1),
        "b_route": jnp.pad(jnp.concatenate([p["b_group"][l], p["b_router"][l]]),
                           (0, LANES - N_GROUPS - N_EXPERTS))[None, :],
        "layer": l,
        "w_exp_gate": p["w_exp_gate"],
        "w_exp_up": p["w_exp_up"],
        "w_exp_down": p["w_exp_down"],
    }


def _pad_rows(a, n, front=False):
    extra = n - a.shape[1]
    return jnp.pad(a, ((0, 0), (extra, 0) if front else (0, extra), (0, 0)))


def _pair_state(s):
    z = jnp.zeros_like(s[:, 0::2])
    top = jnp.concatenate([s[:, 0::2], z], axis=-1)
    bot = jnp.concatenate([z, s[:, 1::2]], axis=-1)
    return jnp.concatenate([top, bot], axis=-2)


def _unpair_state(sp):
    even = sp[:, :, :GDN_DK, :GDN_DV]
    odd = sp[:, :, GDN_DK:, GDN_DV:]
    return jnp.stack([even, odd], axis=2).reshape(sp.shape[0], GDN_HEADS, GDN_DK, GDN_DV)


def _token_mixer(x, lw, state, start_pos):
    B, L, _ = x.shape
    T = B * L
    q, kv, conf_in, gqkv, gz, pool_in, gba = _proj(x.reshape(T, D_MODEL), lw["norm_mix_g"], lw["w_small"])
    seq = lambda a: a.reshape(B, L, a.shape[-1])
    q, kv, conf_in, gqkv, gz, pool_in, gba = map(seq, (q, kv, conf_in, gqkv, gz, pool_in, gba))
    has_past = state is not None
    if has_past:
        past_k, past_v, conf_buf, gdn_buf, gdn_s, pool_buf = state
        kv0 = jnp.concatenate([past_k.reshape(B, WINDOW, SWA_KV), past_v.reshape(B, WINDOW, SWA_KV)], axis=-1)
        ll, lg = STEP_PAD, CHUNK
    else:
        conf_buf = jnp.zeros((B, CONF_KERNEL - 1, CONF_DIM), F32)
        gdn_buf = jnp.zeros((B, GDN_CONV - 1, 3 * GDN_DIM), F32)
        gdn_s = jnp.zeros((B, GDN_HEADS, GDN_DK, GDN_DV), F32)
        pool_buf = jnp.zeros((B, POOL_MAX - 1, POOL_DIM), F32)
        ll = lg = L
    if has_past:
        attn = _attn_step(_pad_rows(q, STEP_PAD), _pad_rows(kv, STEP_PAD), kv0, lw["swa_sinks"])[:, :L]
    else:
        attn = _attn(q, kv, lw["swa_sinks"])
    c, pooled, cu_last = _local_mixers(_pad_rows(conf_in, ll), _pad_rows(pool_in, ll),
                                       _pad_rows(conf_buf, CONF_HIST, front=True),
                                       _pad_rows(pool_buf, POOL_HIST, front=True), lw, start_pos)
    o, s_fin = _gdn(_pad_rows(gqkv, lg), _pad_rows(gz, lg), _pad_rows(gba, lg),
                    _pad_rows(gdn_buf, GDN_HIST, front=True), _pair_state(gdn_s), lw, L)
    if has_past:
        new_kv = jnp.concatenate([kv0[:, L:], kv], axis=1)
        n_hist = cu_last.shape[1]
        new_conf = jnp.concatenate([conf_buf[:, L:], cu_last[:, n_hist - ll:n_hist - ll + L]], axis=1)
        new_gdn_buf = jnp.concatenate([gdn_buf, gqkv], axis=1)[:, -(GDN_CONV - 1):]
        new_pool = jnp.concatenate([pool_buf[:, L:], pool_in], axis=1)
    else:
        new_kv = kv[:, -WINDOW:]
        new_conf = cu_last[:, -(CONF_KERNEL - 1):]
        new_gdn_buf = gqkv[:, -(GDN_CONV - 1):]
        new_pool = pool_in[:, -(POOL_MAX - 1):]
    new_k = new_kv[..., :SWA_KV].reshape(B, WINDOW, SWA_KV_HEADS, HEAD_DIM)
    new_v = new_kv[..., SWA_KV:].reshape(B, WINDOW, SWA_KV_HEADS, HEAD_DIM)
    flat = lambda a: a[:, :L].reshape(T, a.shape[-1])
    branches = (flat(attn), flat(c), flat(o), flat(pooled))
    return branches, (new_k, new_v, new_conf, new_gdn_buf, _unpair_state(s_fin), new_pool)


def _trunk(x, past, start_pos, layers, g_final):
    B, L, _ = x.shape
    collected = [[] for _ in range(6)]
    xf = x.reshape(B * L, D_MODEL)
    for l, lw in enumerate(layers):
        state = None if past is None else tuple(p[l] for p in past)
        branches, new = _token_mixer(xf.reshape(B, L, D_MODEL), lw, state, start_pos)
        x_new, xn2, route_i, route_w, counts = _merge(xf, *branches, lw)
        xf = _moe(x_new, xn2, route_i, route_w, counts, lw, g_final, l == len(layers) - 1)
        for lst, arr in zip(collected, new):
            lst.append(arr)
    return xf.reshape(B, L, D_MODEL), [jnp.stack(lst, axis=0) for lst in collected]


def kernel(x_prompt, x_sample, cache_swa_k, cache_swa_v, state_conf_conv, state_gdn_conv, state_gdn, state_pool, norm_mix_g, w_in, swa_sinks, conf_dw_w, conf_dw_b, conf_ln_g, conf_ln_b, gdn_conv_w, gdn_a_log, gdn_dt_bias, gdn_norm_g, pool_w, pool_scale, w_branch_attn, w_branch_conf, w_branch_gdn, w_branch_pool, w_out, norm_ffn_g, w_group, b_group, w_router, b_router, w_exp_gate, w_exp_up, w_exp_down, final_norm_g):
    params = dict(norm_mix_g=norm_mix_g, w_in=w_in, swa_sinks=swa_sinks, conf_dw_w=conf_dw_w,
                  conf_dw_b=conf_dw_b, conf_ln_g=conf_ln_g, conf_ln_b=conf_ln_b, gdn_conv_w=gdn_conv_w,
                  gdn_a_log=gdn_a_log, gdn_dt_bias=gdn_dt_bias, gdn_norm_g=gdn_norm_g, pool_w=pool_w,
                  pool_scale=pool_scale, w_branch_attn=w_branch_attn, w_branch_conf=w_branch_conf,
                  w_branch_gdn=w_branch_gdn, w_branch_pool=w_branch_pool, w_out=w_out, norm_ffn_g=norm_ffn_g,
                  w_group=w_group, b_group=b_group, w_router=w_router, b_router=b_router,
                  w_exp_gate=w_exp_gate, w_exp_up=w_exp_up, w_exp_down=w_exp_down)
    layers = [_layer_weights(params, l) for l in range(w_in.shape[0])]
    g_final = final_norm_g[None, :]
    yp, (pk, pv, pc, pgc, pgs, pp) = _trunk(x_prompt, None, 0, layers, g_final)
    past = (cache_swa_k, cache_swa_v, state_conf_conv, state_gdn_conv, state_gdn, state_pool)
    ys, (sk, sv, sc, sgc, sgs, sp) = _trunk(x_sample, past, PAST_LEN, layers, g_final)
    return (yp, ys, pk, pv, pc, pgc, pgs, pp, sk, sv, sc, sgc, sgs, sp)
```
